```python
import math
import jax, jax.numpy as jnp
from jax import lax
import numpy as np

D_MODEL = 2048
BATCH = 4
SEQ = 4096
DEPTH = 2

N_A_LAYERS = DEPTH // 2
N_B_LAYERS = DEPTH - N_A_LAYERS

SSM_EXPAND = 2
D_INNER = SSM_EXPAND * D_MODEL
SSM_HEAD_DIM = 64
N_SSM_HEADS = D_INNER // SSM_HEAD_DIM
D_STATE = 128
N_SSM_GROUPS = 8
SSM_CONV = 4
SSD_CHUNK = 128
CONV_DIM = D_INNER + 2 * N_SSM_GROUPS * D_STATE
IN_PROJ_DIM = 2 * D_INNER + 2 * N_SSM_GROUPS * D_STATE + N_SSM_HEADS

N_HEADS = 16
HEAD_DIM = D_MODEL // N_HEADS
MOBA_BLOCK = 256
MOBA_TOPK = 3
Q_BLOCK = 128

D_FF = 11 * D_MODEL // 4
FFN_CONV = 3

DEEPNORM_ALPHA = (2.0 * DEPTH) ** 0.25
DEEPNORM_BETA = (8.0 * DEPTH) ** -0.25
LN_EPS = 1e-5
RMS_EPS = 1e-5

kernel_name = "yoco_mamba2_moba_deepnorm"


def layer_norm(x, g, b):
    xf = x.astype(jnp.float32)
    mu = jnp.mean(xf, axis=-1, keepdims=True)
    var = jnp.mean(jnp.square(xf - mu), axis=-1, keepdims=True)
    return ((xf - mu) * lax.rsqrt(var + LN_EPS) * g + b).astype(x.dtype)


def causal_dwconv(x, w, b):
    K, C = w.shape
    y = lax.conv_general_dilated(x, w[:, None, :].astype(x.dtype), window_strides=(1,),
                                 padding=[(K - 1, 0)],
                                 dimension_numbers=("NWC", "WIO", "NWC"),
                                 feature_group_count=C)
    return y + b


def ssd_chunked(xdt, adt, b, c):
    Bsz, S, H, P = xdt.shape
    G, N = b.shape[2], b.shape[3]
    R = H // G
    nc, l = S // SSD_CHUNK, SSD_CHUNK
    x = xdt.reshape(Bsz, nc, l, G, R, P)
    a = adt.reshape(Bsz, nc, l, G, R)
    b = b.reshape(Bsz, nc, l, G, N)
    c = c.reshape(Bsz, nc, l, G, N)
    a_cum = jnp.cumsum(a, axis=2)
    causal = jnp.tril(jnp.ones((l, l), dtype=bool))[None, None, :, :, None, None]
    seg = a_cum[:, :, :, None] - a_cum[:, :, None, :]
    decay_in = jnp.exp(jnp.where(causal, seg, -jnp.inf))
    cb = jnp.einsum('bclgn,bcsgn->bclsg', c, b)
    y_diag = jnp.einsum('bclsgr,bcsgrp->bclgrp', cb[..., None] * decay_in, x)
    decay_to_end = jnp.exp(a_cum[:, :, -1:] - a_cum)
    chunk_states = jnp.einsum('bclgn,bclgr,bclgrp->bcgrpn', b, decay_to_end, x)
    chunk_decay = jnp.exp(a_cum[:, :, -1])

    def step(state, inp):
        st, dec = inp
        return state * dec[..., None, None] + st, state

    init = jnp.zeros((Bsz, G, R, P, N), dtype=x.dtype)
    _, prev = lax.scan(step, init, (jnp.moveaxis(chunk_states, 1, 0), jnp.moveaxis(chunk_decay, 1, 0)))
    prev = jnp.moveaxis(prev, 0, 1)
    y_off = jnp.einsum('bclgn,bcgrpn,bclgr->bclgrp', c, prev, jnp.exp(a_cum))
    return (y_diag + y_off).reshape(Bsz, S, H, P)


def mamba2_mixer(x, w_in, conv_w, conv_b, dt_bias, a_log, d_skip, norm_w, w_out):
    Bsz, S, _ = x.shape
    zxbcdt = x @ w_in
    z = zxbcdt[..., :D_INNER]
    xbc = zxbcdt[..., D_INNER:D_INNER + CONV_DIM]
    dt = zxbcdt[..., D_INNER + CONV_DIM:]
    xbc = jax.nn.silu(causal_dwconv(xbc, conv_w, conv_b))
    gn = N_SSM_GROUPS * D_STATE
    xs = xbc[..., :D_INNER].astype(jnp.float32).reshape(Bsz, S, N_SSM_HEADS, SSM_HEAD_DIM)
    b_ssm = xbc[..., D_INNER:D_INNER + gn].astype(jnp.float32).reshape(Bsz, S, N_SSM_GROUPS, D_STATE)
    c_ssm = xbc[..., D_INNER + gn:].astype(jnp.float32).reshape(Bsz, S, N_SSM_GROUPS, D_STATE)
    dt = jax.nn.softplus(dt.astype(jnp.float32) + dt_bias.astype(jnp.float32))
    a = -jnp.exp(a_log.astype(jnp.float32))
    y = ssd_chunked(xs * dt[..., None], a * dt, b_ssm, c_ssm)
    y = y + d_skip.astype(jnp.float32)[:, None] * xs
    y = y.reshape(Bsz, S, D_INNER) * jax.nn.silu(z.astype(jnp.float32))
    yg = y.reshape(Bsz, S, N_SSM_GROUPS, D_INNER // N_SSM_GROUPS)
    yg = yg * lax.rsqrt(jnp.mean(jnp.square(yg), axis=-1, keepdims=True) + RMS_EPS)
    y = (yg.reshape(Bsz, S, D_INNER) * norm_w).astype(x.dtype)
    return y @ w_out


def conv_ffn(x, w_up, conv_w, conv_b, w_down):
    u = causal_dwconv(x @ w_up, conv_w, conv_b)
    gate, val = u[..., :D_FF], u[..., D_FF:]
    return (jax.nn.silu(gate) * val) @ w_down


def shared_kv(h, w_kv):
    Bsz, S, _ = h.shape
    nb = -(-S // MOBA_BLOCK)
    pad = nb * MOBA_BLOCK - S
    kv = h @ w_kv

    def to_blocks(t):
        t = jnp.pad(t.reshape(Bsz, S, N_HEADS, HEAD_DIM), ((0, 0), (0, pad), (0, 0), (0, 0)))
        return t.reshape(Bsz, nb, MOBA_BLOCK, N_HEADS, HEAD_DIM).transpose(3, 0, 1, 2, 4)

    k_blk = to_blocks(kv[..., :D_MODEL])
    v_blk = to_blocks(kv[..., D_MODEL:])
    k_mean = jnp.mean(k_blk.astype(jnp.float32), axis=3).astype(k_blk.dtype)
    return k_blk, v_blk, k_mean


def alibi_slopes():
    return 2.0 ** (-8.0 * jnp.arange(1, N_HEADS + 1, dtype=jnp.float32) / N_HEADS)


def moba_cross_attention(h, w_q, w_o, k_blk, v_blk, k_mean):
    Bsz, S, _ = h.shape
    nb = k_blk.shape[2]
    n_sel = max(1, min(MOBA_TOPK, nb))
    scale = HEAD_DIM ** -0.5
    q = (h @ w_q).reshape(Bsz, S, N_HEADS, HEAD_DIM).transpose(2, 0, 1, 3)
    nq = S // Q_BLOCK
    q_chunks = q.reshape(N_HEADS, Bsz, nq, Q_BLOCK, HEAD_DIM).transpose(2, 0, 1, 3, 4)
    slopes = alibi_slopes()
    blk_ids = jnp.arange(nb)
    key_off = jnp.arange(MOBA_BLOCK)
    bidx = jnp.arange(Bsz)[:, None, None]

    def per_chunk(args):
        cidx, qc = args
        t = cidx * Q_BLOCK + jnp.arange(Q_BLOCK)
        own = (cidx * Q_BLOCK) // MOBA_BLOCK

        def per_head(hargs):
            qh, kh, vh, kmh, m = hargs
            gate = jnp.einsum('bqd,bnd->bqn', qh, kmh).astype(jnp.float32)
            gate = jnp.where((blk_ids < own)[None, None, :], gate, -jnp.inf)
            _, top_idx = lax.top_k(gate, n_sel)
            sel_valid = top_idx < own
            kg = kh[bidx, top_idx]
            vg = vh[bidx, top_idx]
            s_sel = jnp.einsum('bqd,bqjkd->bqjk', qh, kg).astype(jnp.float32) * scale
            pos_sel = top_idx[..., None] * MOBA_BLOCK + key_off
            dist_sel = (t[None, :, None, None] - pos_sel).astype(jnp.float32)
            s_sel = jnp.where(sel_valid[..., None], s_sel - m * dist_sel, -jnp.inf)
            ko = lax.dynamic_index_in_dim(kh, own, axis=1, keepdims=False)
            vo = lax.dynamic_index_in_dim(vh, own, axis=1, keepdims=False)
            s_own = jnp.einsum('bqd,bkd->bqk', qh, ko).astype(jnp.float32) * scale
            dist_own = t[:, None] - (own * MOBA_BLOCK + key_off)[None, :]
            s_own = jnp.where((dist_own >= 0)[None], s_own - m * dist_own.astype(jnp.float32)[None], -jnp.inf)
            scores = jnp.concatenate([s_sel.reshape(Bsz, Q_BLOCK, n_sel * MOBA_BLOCK), s_own], axis=-1)
            p = jax.nn.softmax(scores, axis=-1)
            p_sel = p[..., :n_sel * MOBA_BLOCK].reshape(Bsz, Q_BLOCK, n_sel, MOBA_BLOCK).astype(vh.dtype)
            p_own = p[..., n_sel * MOBA_BLOCK:].astype(vh.dtype)
            return (jnp.einsum('bqjk,bqjkd->bqd', p_sel, vg)
                    + jnp.einsum('bqk,bkd->bqd', p_own, vo))

        return lax.map(per_head, (qc, k_blk, v_blk, k_mean, slopes))

    out = lax.map(per_chunk, (jnp.arange(nq), q_chunks))
    out = out.transpose(2, 0, 3, 1, 4).reshape(Bsz, S, N_HEADS * HEAD_DIM)
    return out @ w_o


def setup_inputs(seed: int = 0) -> dict:
    key = jax.random.key(seed)
    ks = jax.random.split(key, 24)
    f32 = jnp.float32
    nrm = lambda k, shape, fan_in: jax.random.normal(k, shape, f32) * (fan_in ** -0.5)
    dt0 = jnp.exp(jax.random.uniform(ks[4], (N_A_LAYERS, N_SSM_HEADS), f32)
                  * (math.log(0.1) - math.log(0.001)) + math.log(0.001))
    return {
        "x": jax.random.normal(ks[0], (BATCH, SEQ, D_MODEL), f32),
        "mamba_w_in": nrm(ks[1], (N_A_LAYERS, D_MODEL, IN_PROJ_DIM), D_MODEL),
        "mamba_conv_w": nrm(ks[2], (N_A_LAYERS, SSM_CONV, CONV_DIM), SSM_CONV),
        "mamba_conv_b": 0.02 * jax.random.normal(ks[3], (N_A_LAYERS, CONV_DIM), f32),
        "mamba_dt_bias": dt0 + jnp.log(-jnp.expm1(-dt0)),
        "mamba_a_log": jnp.log(jax.random.uniform(ks[5], (N_A_LAYERS, N_SSM_HEADS), f32, 1.0, 16.0)),
        "mamba_d": 1.0 + 0.02 * jax.random.normal(ks[6], (N_A_LAYERS, N_SSM_HEADS), f32),
        "mamba_norm_w": 1.0 + 0.02 * jax.random.normal(ks[7], (N_A_LAYERS, D_INNER), f32),
        "mamba_w_out": DEEPNORM_BETA * nrm(ks[8], (N_A_LAYERS, D_INNER, D_MODEL), D_INNER),
        "ffn_w_up": nrm(ks[9], (DEPTH, D_MODEL, 2 * D_FF), D_MODEL),
        "ffn_conv_w": nrm(ks[10], (DEPTH, FFN_CONV, 2 * D_FF), FFN_CONV),
        "ffn_conv_b": 0.02 * jax.random.normal(ks[11], (DEPTH, 2 * D_FF), f32),
        "ffn_w_down": DEEPNORM_BETA * nrm(ks[12], (DEPTH, D_FF, D_MODEL), D_FF),
        "ln_mix_g": 1.0 + 0.02 * jax.random.normal(ks[13], (DEPTH, D_MODEL), f32),
        "ln_mix_b": 0.02 * jax.random.normal(ks[14], (DEPTH, D_MODEL), f32),
        "ln_ffn_g": 1.0 + 0.02 * jax.random.normal(ks[15], (DEPTH, D_MODEL), f32),
        "ln_ffn_b": 0.02 * jax.random.normal(ks[16], (DEPTH, D_MODEL), f32),
        "w_kv": jnp.concatenate([nrm(ks[17], (D_MODEL, D_MODEL), D_MODEL),
                                 DEEPNORM_BETA * nrm(ks[18], (D_MODEL, D_MODEL), D_MODEL)], axis=1),
        "attn_w_q": nrm(ks[19], (N_B_LAYERS, D_MODEL, N_HEADS * HEAD_DIM), D_MODEL),
        "attn_w_o": DEEPNORM_BETA * nrm(ks[20], (N_B_LAYERS, N_HEADS * HEAD_DIM, D_MODEL), N_HEADS * HEAD_DIM),
    }


def reference(x, mamba_w_in, mamba_conv_w, mamba_conv_b, mamba_dt_bias, mamba_a_log, mamba_d,
              mamba_norm_w, mamba_w_out, ffn_w_up, ffn_conv_w, ffn_conv_b, ffn_w_down,
              ln_mix_g, ln_mix_b, ln_ffn_g, ln_ffn_b, w_kv, attn_w_q, attn_w_o):
    h = x
    k_blk = v_blk = k_mean = None
    for layer in range(DEPTH):
        if layer < N_A_LAYERS:
            mix = mamba2_mixer(h, mamba_w_in[layer], mamba_conv_w[layer], mamba_conv_b[layer],
                               mamba_dt_bias[layer], mamba_a_log[layer], mamba_d[layer],
                               mamba_norm_w[layer], mamba_w_out[layer])
        else:
            j = layer - N_A_LAYERS
            mix = moba_cross_attention(h, attn_w_q[j], attn_w_o[j], k_blk, v_blk, k_mean)
        h = layer_norm(DEEPNORM_ALPHA * h + mix, ln_mix_g[layer], ln_mix_b[layer])
        ffn = conv_ffn(h, ffn_w_up[layer], ffn_conv_w[layer], ffn_conv_b[layer], ffn_w_down[layer])
        h = layer_norm(DEEPNORM_ALPHA * h + ffn, ln_ffn_g[layer], ln_ffn_b[layer])
        if layer == N_A_LAYERS - 1:
            k_blk, v_blk, k_mean = shared_kv(h, w_kv)
    return h
```

```python
import functools
import math

import jax
import jax.numpy as jnp
from jax import lax
from jax.experimental import pallas as pl
from jax.experimental.pallas import tpu as pltpu

F32 = jnp.float32
BF16 = jnp.bfloat16

SSM_HEAD_DIM = 64
D_STATE = 128
N_SSM_GROUPS = 8
SSD_CHUNK = 128
HEAD_DIM = 128
MOBA_BLOCK = 256
MOBA_TOPK = 3
LN_EPS = 1e-5
RMS_EPS = 1e-5

V7X_LANES = 128
V7X_SUBLANES = 8
V7X_VMEM_LIMIT_BYTES = 56 * 1024 * 1024


def _tile(dim, target, unit=V7X_LANES):
    if dim <= target:
        return dim
    best = None
    t = unit
    while t <= target:
        if dim % t == 0:
            best = t
        t += unit
    assert best is not None, (dim, target, unit)
    return best


def _params(semantics):
    return pltpu.CompilerParams(dimension_semantics=semantics,
                                vmem_limit_bytes=V7X_VMEM_LIMIT_BYTES)


def _dot(a, b):
    return jnp.dot(a, b, preferred_element_type=F32)


def _dot_nt(a, b):
    return lax.dot_general(a, b, (((1,), (1,)), ((), ())), preferred_element_type=F32)


def _split3(v):
    hi = v.astype(BF16)
    r1 = v - hi.astype(F32)
    mid = r1.astype(BF16)
    lo = (r1 - mid.astype(F32)).astype(BF16)
    return hi, mid, lo


def _silu(v):
    return v * jax.nn.sigmoid(v)


def _mm_kernel(x_ref, w_ref, o_ref):
    o_ref[...] = _dot(x_ref[...], w_ref[...]).astype(o_ref.dtype)


def _matmul(x, w, out_dtype, *, tm=1024, tn=1024, name):
    M, K = x.shape
    N = w.shape[1]
    tm = _tile(M, tm)
    tn = _tile(N, tn)
    return pl.pallas_call(
        _mm_kernel,
        out_shape=jax.ShapeDtypeStruct((M, N), out_dtype),
        grid=(M // tm, N // tn),
        in_specs=[pl.BlockSpec((tm, K), lambda m, n: (m, 0)),
                  pl.BlockSpec((K, tn), lambda m, n: (0, n))],
        out_specs=pl.BlockSpec((tm, tn), lambda m, n: (m, n)),
        compiler_params=_params(("parallel", "arbitrary")),
        name=name,
    )(x, w)


def _mm_ln_kernel(x_ref, w_ref, h_ref, g_ref, b_ref, of_ref, ob_ref, acc_ref, *, nk, alpha):
    k = pl.program_id(1)

    @pl.when(k == 0)
    def _():
        acc_ref[...] = jnp.zeros_like(acc_ref)

    acc_ref[...] += _dot(x_ref[...], w_ref[...])

    @pl.when(k == nk - 1)
    def _():
        y = alpha * h_ref[...] + acc_ref[...]
        mu = jnp.mean(y, axis=-1, keepdims=True)
        d = y - mu
        var = jnp.mean(d * d, axis=-1, keepdims=True)
        out = d * lax.rsqrt(var + LN_EPS) * g_ref[...] + b_ref[...]
        of_ref[...] = out
        ob_ref[...] = out.astype(BF16)


def _matmul_res_ln(x, w, h, g, b, alpha, *, tm=512, tk=1536, name):
    M, K = x.shape
    N = w.shape[1]
    tm = _tile(M, tm)
    tk = _tile(K, tk)
    nk = K // tk
    return pl.pallas_call(
        functools.partial(_mm_ln_kernel, nk=nk, alpha=alpha),
        out_shape=(jax.ShapeDtypeStruct((M, N), F32), jax.ShapeDtypeStruct((M, N), BF16)),
        grid=(M // tm, nk),
        in_specs=[pl.BlockSpec((tm, tk), lambda m, k: (m, k)),
                  pl.BlockSpec((tk, N), lambda m, k: (k, 0)),
                  pl.BlockSpec((tm, N), lambda m, k: (m, 0)),
                  pl.BlockSpec((1, N), lambda m, k: (0, 0)),
                  pl.BlockSpec((1, N), lambda m, k: (0, 0))],
        out_specs=(pl.BlockSpec((tm, N), lambda m, k: (m, 0)),
                   pl.BlockSpec((tm, N), lambda m, k: (m, 0))),
        scratch_shapes=[pltpu.VMEM((tm, N), F32)],
        compiler_params=_params(("parallel", "arbitrary")),
        name=name,
    )(x, w, h, g.reshape(1, N), b.reshape(1, N))


def _shift_rows(u, tail, k):
    rolled = pltpu.roll(u, k, axis=0)
    tail_rolled = pltpu.roll(tail, k, axis=0)
    row = lax.broadcasted_iota(jnp.int32, tail.shape, 0)
    top = jnp.where(row < k, tail_rolled, rolled[:V7X_SUBLANES])
    return jnp.concatenate([top, rolled[V7X_SUBLANES:]], axis=0)


def _causal_dwconv(u, tail, w_ref, b_ref):
    K = w_ref.shape[0]
    out = b_ref[...] + w_ref[K - 1:K, :] * u
    for k in range(1, K):
        out = out + w_ref[K - 1 - k:K - k, :] * _shift_rows(u, tail, k)
    return out


def _ffn_up_kernel(x_ref, wg_ref, wv_ref, cwg_ref, cwv_ref, cbg_ref, cbv_ref, o_ref,
                   tailg_ref, tailv_ref, *, tiles_per_seq):
    m = pl.program_id(1)

    @pl.when(m % tiles_per_seq == 0)
    def _():
        tailg_ref[...] = jnp.zeros_like(tailg_ref)
        tailv_ref[...] = jnp.zeros_like(tailv_ref)

    x = x_ref[...]
    ug = _dot(x, wg_ref[...])
    uv = _dot(x, wv_ref[...])
    gate = _causal_dwconv(ug, tailg_ref[...], cwg_ref, cbg_ref)
    val = _causal_dwconv(uv, tailv_ref[...], cwv_ref, cbv_ref)
    tailg_ref[...] = ug[-V7X_SUBLANES:]
    tailv_ref[...] = uv[-V7X_SUBLANES:]
    o_ref[...] = (_silu(gate) * val).astype(o_ref.dtype)


def _ffn_up(x, w_up, conv_w, conv_b, seq, *, tm=1024, tn=512, name):
    M, K = x.shape
    d_ff = w_up.shape[1] // 2
    tm = _tile(seq, tm, V7X_SUBLANES)
    tn = _tile(d_ff, tn)
    nn = d_ff // tn
    kc = conv_w.shape[0]
    conv_b = conv_b.reshape(1, 2 * d_ff)
    return pl.pallas_call(
        functools.partial(_ffn_up_kernel, tiles_per_seq=seq // tm),
        out_shape=jax.ShapeDtypeStruct((M, d_ff), BF16),
        grid=(nn, M // tm),
        in_specs=[pl.BlockSpec((tm, K), lambda n, m: (m, 0)),
                  pl.BlockSpec((K, tn), lambda n, m: (0, n)),
                  pl.BlockSpec((K, tn), lambda n, m: (0, n + nn)),
                  pl.BlockSpec((kc, tn), lambda n, m: (0, n)),
                  pl.BlockSpec((kc, tn), lambda n, m: (0, n + nn)),
                  pl.BlockSpec((1, tn), lambda n, m: (0, n)),
                  pl.BlockSpec((1, tn), lambda n, m: (0, n + nn))],
        out_specs=pl.BlockSpec((tm, tn), lambda n, m: (m, n)),
        scratch_shapes=[pltpu.VMEM((V7X_SUBLANES, tn), F32),
                        pltpu.VMEM((V7X_SUBLANES, tn), F32)],
        compiler_params=_params(("parallel", "arbitrary")),
        name=name,
    )(x, w_up, w_up, conv_w, conv_w, conv_b, conv_b)


def _ssd_kernel(z_ref, xs_ref, bc_ref, dt_ref, cwx_ref, cwbc_ref, cbx_ref, cbbc_ref,
                dtb_ref, alog_ref, dskip_ref, nw_ref, expand_ref, o_ref,
                tailx_ref, tailbc_ref, state_ref, acum_t_ref, y_ref, *, heads_per_group):
    c = pl.program_id(1)
    L = SSD_CHUNK
    P = SSM_HEAD_DIM
    N = D_STATE
    G = N_SSM_GROUPS
    R = heads_per_group
    H = G * R
    GP = R * P

    @pl.when(c == 0)
    def _():
        tailx_ref[...] = jnp.zeros_like(tailx_ref)
        tailbc_ref[...] = jnp.zeros_like(tailbc_ref)
        state_ref[...] = jnp.zeros_like(state_ref)

    x_raw = xs_ref[...].astype(F32)
    bc_raw = bc_ref[...].astype(F32)
    xs = _silu(_causal_dwconv(x_raw, tailx_ref[...], cwx_ref, cbx_ref))
    bc = _silu(_causal_dwconv(bc_raw, tailbc_ref[...], cwbc_ref, cbbc_ref))
    tailx_ref[...] = x_raw[-V7X_SUBLANES:]
    tailbc_ref[...] = bc_raw[-V7X_SUBLANES:]

    dt = jax.nn.softplus(dt_ref[...] + dtb_ref[...])
    adt = dt * (-jnp.exp(alog_ref[...]))

    row = lax.broadcasted_iota(jnp.int32, (L, L), 0)
    col = lax.broadcasted_iota(jnp.int32, (L, L), 1)
    causal = row >= col
    tril = jnp.where(causal, 1.0, 0.0).astype(BF16)
    a_hi, a_mid, a_lo = _split3(adt)
    acum = _dot(tril, a_hi) + _dot(tril, a_mid) + _dot(tril, a_lo)

    eye = jnp.where(lax.broadcasted_iota(jnp.int32, (H, H), 0)
                    == lax.broadcasted_iota(jnp.int32, (H, H), 1), 1.0, 0.0).astype(BF16)
    c_hi, c_mid, c_lo = _split3(acum)
    acum_t_ref[...] = _dot_nt(eye, c_hi) + _dot_nt(eye, c_mid) + _dot_nt(eye, c_lo)

    acum_last = acum[L - 1:L, :]
    decay_to_end = jnp.exp(acum_last - acum)
    decay_from_start = jnp.exp(acum)
    chunk_decay = jnp.broadcast_to(jnp.exp(acum_last), (V7X_SUBLANES, H))

    stacked = jnp.concatenate([dt, decay_to_end, decay_from_start, chunk_decay], axis=0)
    s_hi, s_mid, s_lo = _split3(stacked)
    expand = expand_ref[...]
    wide = _dot(s_hi, expand) + _dot(s_mid, expand) + _dot(s_lo, expand)
    dt_w = wide[0:L]
    dte_w = wide[L:2 * L]
    dfs_w = wide[2 * L:3 * L]
    cdecay_w = wide[3 * L:3 * L + 1]

    xdt = xs * dt_w
    xdt_bf = xdt.astype(BF16)
    xend_bf = (xdt * dte_w).astype(BF16)

    for g in range(G):
        gs = slice(g * GP, (g + 1) * GP)
        b_g = bc[:, g * N:(g + 1) * N]
        c_g = bc[:, G * N + g * N:G * N + (g + 1) * N].astype(BF16)
        b_gt = b_g.T.astype(BF16)
        cb = _dot(c_g, b_gt)
        state_g = state_ref[:, gs]
        y_off = _dot(c_g, state_g.astype(BF16)) * dfs_w[:, gs]
        state_ref[:, gs] = state_g * cdecay_w[:, gs] + _dot(b_gt, xend_bf[:, gs])
        for r in range(R):
            h = g * R + r
            seg = acum[:, h:h + 1] - acum_t_ref[h:h + 1, :]
            m = cb * jnp.exp(jnp.where(causal, seg, -jnp.inf))
            y_ref[:, h * P:(h + 1) * P] = _dot(m.astype(BF16), xdt_bf[:, h * P:(h + 1) * P])
        y = y_ref[:, gs] + y_off + dskip_ref[:, gs] * xs[:, gs]
        y = y * _silu(z_ref[:, gs].astype(F32))
        ms = jnp.mean(y * y, axis=-1, keepdims=True)
        o_ref[:, gs] = (y * lax.rsqrt(ms + RMS_EPS) * nw_ref[:, gs]).astype(o_ref.dtype)


def _ssd(zx, dt_raw, conv_w, conv_b, dt_bias, a_log, d_skip, norm_w, batch, seq, *, name):
    M = zx.shape[0]
    d_inner = norm_w.shape[0]
    H = dt_bias.shape[0]
    R = H // N_SSM_GROUPS
    gn2 = 2 * N_SSM_GROUPS * D_STATE
    assert d_inner == H * SSM_HEAD_DIM and d_inner % gn2 == 0 and seq % SSD_CHUNK == 0
    L = SSD_CHUNK
    nc = seq // L
    kc = conv_w.shape[0]
    conv_b = conv_b.reshape(1, -1)
    expand = jnp.repeat(jnp.eye(H, dtype=BF16), SSM_HEAD_DIM, axis=1)
    d_wide = jnp.repeat(d_skip.astype(F32), SSM_HEAD_DIM).reshape(1, d_inner)
    bc_blk = 2 * d_inner // gn2
    row = lambda b, c: b * nc + c
    const = lambda b, c: (0, 0)
    return pl.pallas_call(
        functools.partial(_ssd_kernel, heads_per_group=R),
        out_shape=jax.ShapeDtypeStruct((M, d_inner), BF16),
        grid=(batch, nc),
        in_specs=[pl.BlockSpec((L, d_inner), lambda b, c: (row(b, c), 0)),
                  pl.BlockSpec((L, d_inner), lambda b, c: (row(b, c), 1)),
                  pl.BlockSpec((L, gn2), lambda b, c: (row(b, c), bc_blk)),
                  pl.BlockSpec((L, H), lambda b, c: (row(b, c), 0)),
                  pl.BlockSpec((kc, d_inner), const),
                  pl.BlockSpec((kc, gn2), lambda b, c: (0, d_inner // gn2)),
                  pl.BlockSpec((1, d_inner), const),
                  pl.BlockSpec((1, gn2), lambda b, c: (0, d_inner // gn2)),
                  pl.BlockSpec((1, H), const),
                  pl.BlockSpec((1, H), const),
                  pl.BlockSpec((1, d_inner), const),
                  pl.BlockSpec((1, d_inner), const),
                  pl.BlockSpec((H, d_inner), const)],
        out_specs=pl.BlockSpec((L, d_inner), lambda b, c: (row(b, c), 0)),
        scratch_shapes=[pltpu.VMEM((V7X_SUBLANES, d_inner), F32),
                        pltpu.VMEM((V7X_SUBLANES, gn2), F32),
                        pltpu.VMEM((D_STATE, d_inner), F32),
                        pltpu.VMEM((H, L), F32),
                        pltpu.VMEM((L, d_inner), F32)],
        compiler_params=_params(("parallel", "arbitrary")),
        name=name,
    )(zx, zx, zx, dt_raw, conv_w, conv_w, conv_b, conv_b,
      dt_bias.reshape(1, H).astype(F32), a_log.reshape(1, H).astype(F32), d_wide,
      norm_w.reshape(1, d_inner).astype(F32), expand)


def _k_proj_kernel(x_ref, w_ref, k_ref, km_ref):
    acc = _dot(x_ref[...], w_ref[...])
    k_ref[...] = acc.astype(k_ref.dtype)
    nb, tn = km_ref.shape[1], km_ref.shape[2]
    km_ref[0] = jnp.mean(acc.reshape(nb, MOBA_BLOCK, tn), axis=1)


def _k_proj(x, w, *, tm=1024, tn=1024, name):
    M, K = x.shape
    N = w.shape[1]
    tm = _tile(M, tm, MOBA_BLOCK)
    tn = _tile(N, tn)
    nb = tm // MOBA_BLOCK
    k, km = pl.pallas_call(
        _k_proj_kernel,
        out_shape=(jax.ShapeDtypeStruct((M, N), BF16),
                   jax.ShapeDtypeStruct((M // tm, nb, N), F32)),
        grid=(M // tm, N // tn),
        in_specs=[pl.BlockSpec((tm, K), lambda m, n: (m, 0)),
                  pl.BlockSpec((K, tn), lambda m, n: (0, n))],
        out_specs=(pl.BlockSpec((tm, tn), lambda m, n: (m, n)),
                   pl.BlockSpec((1, nb, tn), lambda m, n: (m, 0, n))),
        compiler_params=_params(("parallel", "arbitrary")),
        name=name,
    )(x, w)
    return k, km.reshape(M // MOBA_BLOCK, N)


def _moba_kernel(q_ref, k_ref, v_ref, km_ref, slope_ref, o_ref, *, n_blocks):
    i = pl.program_id(2)
    T = MOBA_BLOCK
    q = q_ref[0]
    slope = slope_ref[0]
    slope_col = slope[:, :1]
    scale = HEAD_DIM ** -0.5

    gate = _dot_nt(q, km_ref[0].astype(BF16))
    blk = lax.broadcasted_iota(jnp.int32, gate.shape, 1)
    valid = blk < i
    neg_inf = jnp.float32(-jnp.inf)
    chosen = jnp.zeros(gate.shape, dtype=jnp.bool_)
    for _ in range(min(MOBA_TOPK, n_blocks)):
        cur = jnp.where(jnp.logical_and(valid, jnp.logical_not(chosen)), gate, neg_inf)
        best = jnp.max(cur, axis=-1, keepdims=True)
        hit = jnp.logical_and(cur == best, cur > neg_inf)
        first = jnp.min(jnp.where(hit, blk, n_blocks), axis=-1, keepdims=True)
        chosen = jnp.logical_or(chosen, blk == first)
    chosen_f = jnp.where(chosen, 1.0, 0.0)

    rowq = lax.broadcasted_iota(jnp.int32, (T, T), 0)
    colk = lax.broadcasted_iota(jnp.int32, (T, T), 1)
    rel = (rowq - colk).astype(F32)

    k_own = k_ref[0, pl.ds(pl.multiple_of(i * T, T), T), :]
    v_own = v_ref[0, pl.ds(pl.multiple_of(i * T, T), T), :]
    s = _dot_nt(q, k_own) * scale - slope_col * rel
    s = jnp.where(rowq >= colk, s, neg_inf)
    m_run = jnp.max(s, axis=-1, keepdims=True)
    p = jnp.exp(s - m_run)
    l_run = jnp.sum(p, axis=-1, keepdims=True)
    acc = _dot(p.astype(BF16), v_own)

    def body(j, carry):
        m_run, l_run, acc = carry
        k_j = k_ref[0, pl.ds(pl.multiple_of(j * T, T), T), :]
        v_j = v_ref[0, pl.ds(pl.multiple_of(j * T, T), T), :]
        picked = jnp.sum(jnp.where(blk == j, chosen_f, 0.0), axis=-1, keepdims=True) > 0.0
        dist = rel + ((i - j) * T).astype(F32)
        s = _dot_nt(q, k_j) * scale - slope_col * dist
        s = jnp.where(picked, s, neg_inf)
        m_new = jnp.maximum(m_run, jnp.max(s, axis=-1, keepdims=True))
        a = jnp.exp(m_run - m_new)
        p = jnp.exp(s - m_new)
        l_new = a * l_run + jnp.sum(p, axis=-1, keepdims=True)
        acc_new = a * acc + _dot(p.astype(BF16), v_j)
        return m_new, l_new, acc_new

    m_run, l_run, acc = lax.fori_loop(0, i, body, (m_run, l_run, acc))
    o_ref[0] = (acc / l_run).astype(o_ref.dtype)


def _moba(q, k, v, kmean, batch, seq, *, name):
    D = q.shape[-1]
    n_heads = D // HEAD_DIM
    nb = seq // MOBA_BLOCK
    slopes = 2.0 ** (-8.0 * jnp.arange(1, n_heads + 1, dtype=F32) / n_heads)
    slopes = jnp.broadcast_to(slopes[:, None, None], (n_heads, 1, V7X_LANES))
    return pl.pallas_call(
        functools.partial(_moba_kernel, n_blocks=nb),
        out_shape=jax.ShapeDtypeStruct((batch, seq, D), BF16),
        grid=(batch, n_heads, nb),
        in_specs=[pl.BlockSpec((1, MOBA_BLOCK, HEAD_DIM), lambda b, h, i: (b, i, h)),
                  pl.BlockSpec((1, seq, HEAD_DIM), lambda b, h, i: (b, 0, h)),
                  pl.BlockSpec((1, seq, HEAD_DIM), lambda b, h, i: (b, 0, h)),
                  pl.BlockSpec((1, nb, HEAD_DIM), lambda b, h, i: (b, 0, h)),
                  pl.BlockSpec((1, 1, V7X_LANES), lambda b, h, i: (h, 0, 0))],
        out_specs=pl.BlockSpec((1, MOBA_BLOCK, HEAD_DIM), lambda b, h, i: (b, i, h)),
        compiler_params=_params(("parallel", "parallel", "arbitrary")),
        name=name,
    )(q, k, v, kmean, slopes)


def _conv_ffn_block(h_f32, h_bf, w_up, conv_w, conv_b, w_down, g, b, alpha, seq, tag):
    act = _ffn_up(h_bf, w_up.astype(BF16), conv_w, conv_b, seq, name=f"ffn_up_{tag}")
    return _matmul_res_ln(act, w_down.astype(BF16), h_f32, g, b, alpha, tk=1408,
                          name=f"ffn_down_ln_{tag}")


def kernel(x, mamba_w_in, mamba_conv_w, mamba_conv_b, mamba_dt_bias, mamba_a_log, mamba_d,
           mamba_norm_w, mamba_w_out, ffn_w_up, ffn_conv_w, ffn_conv_b, ffn_w_down,
           ln_mix_g, ln_mix_b, ln_ffn_g, ln_ffn_b, w_kv, attn_w_q, attn_w_o):
    batch, seq, d_model = x.shape
    depth = ffn_w_up.shape[0]
    n_a = mamba_w_in.shape[0]
    alpha = (2.0 * depth) ** 0.25
    M = batch * seq

    h = x.reshape(M, d_model)
    h_bf = h.astype(BF16)
    k = v = kmean = None
    for layer in range(depth):
        if layer < n_a:
            d_inner = mamba_norm_w.shape[1]
            n_zx = mamba_w_in.shape[2] - mamba_dt_bias.shape[1]
            w_in = mamba_w_in[layer]
            zx = _matmul(h_bf, w_in[:, :n_zx].astype(BF16), F32, name=f"in_proj_{layer}")
            dt_raw = _matmul(h_bf, w_in[:, n_zx:].astype(BF16), F32, name=f"dt_proj_{layer}")
            y = _ssd(zx, dt_raw, mamba_conv_w[layer], mamba_conv_b[layer], mamba_dt_bias[layer],
                     mamba_a_log[layer], mamba_d[layer], mamba_norm_w[layer], batch, seq,
                     name=f"ssd_{layer}")
            mix_in, w_mix = y, mamba_w_out[layer]
        else:
            j = layer - n_a
            q = _matmul(h_bf, attn_w_q[j].astype(BF16), BF16, name=f"q_proj_{j}")
            o = _moba(q.reshape(batch, seq, d_model), k, v, kmean, batch, seq, name=f"moba_{j}")
            mix_in, w_mix = o.reshape(M, d_model), attn_w_o[j]
        h, h_bf = _matmul_res_ln(mix_in, w_mix.astype(BF16), h, ln_mix_g[layer], ln_mix_b[layer],
                                 alpha, tk=2048, name=f"mix_out_ln_{layer}")
        h, h_bf = _conv_ffn_block(h, h_bf, ffn_w_up[layer], ffn_conv_w[layer], ffn_conv_b[layer],
                                  ffn_w_down[layer], ln_ffn_g[layer], ln_ffn_b[layer], alpha, seq,
                                  layer)
        if layer == n_a - 1:
            k, kmean = _k_proj(h_bf, w_kv[:, :d_model].astype(BF16), name="k_proj")
            v = _matmul(h_bf, w_kv[:, d_model:].astype(BF16), BF16, name="v_proj")
            k = k.reshape(batch, seq, d_model)
            v = v.reshape(batch, seq, d_model)
            kmean = kmean.reshape(batch, seq // MOBA_BLOCK, d_model)
    return h.reshape(batch, seq, d_model)
```

```python
import functools
import math

import jax
import jax.numpy as jnp
from jax import lax
from jax.experimental import pallas as pl
from jax.experimental.pallas import tpu as pltpu

F32 = jnp.float32
BF16 = jnp.bfloat16

SSM_HEAD_DIM = 64
D_STATE = 128
N_SSM_GROUPS = 8
SSD_CHUNK = 128
HEAD_DIM = 128
MOBA_BLOCK = 256
MOBA_TOPK = 3
LN_EPS = 1e-5
RMS_EPS = 1e-5

V7X_LANES = 128
V7X_SUBLANES = 8
V7X_VMEM_LIMIT_BYTES = 56 * 1024 * 1024


def _tile(dim, target, unit=V7X_LANES):
    if dim <= target:
        return dim
    best = None
    t = unit
    while t <= target:
        if dim % t == 0:
            best = t
        t += unit
    assert best is not None, (dim, target, unit)
    return best


def _params(semantics):
    return pltpu.CompilerParams(dimension_semantics=semantics,
                                vmem_limit_bytes=V7X_VMEM_LIMIT_BYTES)


def _dot(a, b):
    return jnp.dot(a, b, preferred_element_type=F32)


def _dot_nt(a, b):
    return lax.dot_general(a, b, (((1,), (1,)), ((), ())), preferred_element_type=F32)


def _split3(v):
    hi = v.astype(BF16)
    r1 = v - hi.astype(F32)
    mid = r1.astype(BF16)
    lo = (r1 - mid.astype(F32)).astype(BF16)
    return hi, mid, lo


def _silu(v):
    return v * jax.nn.sigmoid(v)


def _mm_kernel(x_ref, w_ref, o_ref):
    o_ref[...] = _dot(x_ref[...], w_ref[...]).astype(o_ref.dtype)


def _matmul(x, w, out_dtype, *, tm=1024, tn=1024, name):
    M, K = x.shape
    N = w.shape[1]
    tm = _tile(M, tm)
    tn = _tile(N, tn)
    return pl.pallas_call(
        _mm_kernel,
        out_shape=jax.ShapeDtypeStruct((M, N), out_dtype),
        grid=(M // tm, N // tn),
        in_specs=[pl.BlockSpec((tm, K), lambda m, n: (m, 0)),
                  pl.BlockSpec((K, tn), lambda m, n: (0, n))],
        out_specs=pl.BlockSpec((tm, tn), lambda m, n: (m, n)),
        compiler_params=_params(("parallel", "arbitrary")),
        name=name,
    )(x, w)


def _mm_ln_kernel(x_ref, w_ref, h_ref, g_ref, b_ref, of_ref, ob_ref, acc_ref, *, nk, alpha):
    k = pl.program_id(1)

    @pl.when(k == 0)
    def _():
        acc_ref[...] = jnp.zeros_like(acc_ref)

    acc_ref[...] += _dot(x_ref[...], w_ref[...])

    @pl.when(k == nk - 1)
    def _():
        y = alpha * h_ref[...] + acc_ref[...]
        mu = jnp.mean(y, axis=-1, keepdims=True)
        d = y - mu
        var = jnp.mean(d * d, axis=-1, keepdims=True)
        out = d * lax.rsqrt(var + LN_EPS) * g_ref[...] + b_ref[...]
        of_ref[...] = out
        ob_ref[...] = out.astype(BF16)


def _matmul_res_ln(x, w, h, g, b, alpha, *, tm=512, tk=1536, name):
    M, K = x.shape
    N = w.shape[1]
    tm = _tile(M, tm)
    tk = _tile(K, tk)
    nk = K // tk
    return pl.pallas_call(
        functools.partial(_mm_ln_kernel, nk=nk, alpha=alpha),
        out_shape=(jax.ShapeDtypeStruct((M, N), F32), jax.ShapeDtypeStruct((M, N), BF16)),
        grid=(M // tm, nk),
        in_specs=[pl.BlockSpec((tm, tk), lambda m, k: (m, k)),
                  pl.BlockSpec((tk, N), lambda m, k: (k, 0)),
                  pl.BlockSpec((tm, N), lambda m, k: (m, 0)),
                  pl.BlockSpec((1, N), lambda m, k: (0, 0)),
                  pl.BlockSpec((1, N), lambda m, k: (0, 0))],
        out_specs=(pl.BlockSpec((tm, N), lambda m, k: (m, 0)),
                   pl.BlockSpec((tm, N), lambda m, k: (m, 0))),
        scratch_shapes=[pltpu.VMEM((tm, N), F32)],
        compiler_params=_params(("parallel", "arbitrary")),
        name=name,
    )(x, w, h, g.reshape(1, N), b.reshape(1, N))


def _shift_rows(u, tail, k):
    rolled = pltpu.roll(u, k, axis=0)
    tail_rolled = pltpu.roll(tail, k, axis=0)
    row = lax.broadcasted_iota(jnp.int32, tail.shape, 0)
    top = jnp.where(row < k, tail_rolled, rolled[:V7X_SUBLANES])
    return jnp.concatenate([top, rolled[V7X_SUBLANES:]], axis=0)


def _causal_dwconv(u, tail, w_ref, b_ref):
    K = w_ref.shape[0]
    out = b_ref[...] + w_ref[K - 1:K, :] * u
    for k in range(1, K):
        out = out + w_ref[K - 1 - k:K - k, :] * _shift_rows(u, tail, k)
    return out


def _ffn_up_kernel(x_ref, wg_ref, wv_ref, cwg_ref, cwv_ref, cbg_ref, cbv_ref, o_ref,
                   tailg_ref, tailv_ref, *, tiles_per_seq):
    m = pl.program_id(1)

    @pl.when(m % tiles_per_seq == 0)
    def _():
        tailg_ref[...] = jnp.zeros_like(tailg_ref)
        tailv_ref[...] = jnp.zeros_like(tailv_ref)

    x = x_ref[...]
    ug = _dot(x, wg_ref[...])
    uv = _dot(x, wv_ref[...])
    gate = _causal_dwconv(ug, tailg_ref[...], cwg_ref, cbg_ref)
    val = _causal_dwconv(uv, tailv_ref[...], cwv_ref, cbv_ref)
    tailg_ref[...] = ug[-V7X_SUBLANES:]
    tailv_ref[...] = uv[-V7X_SUBLANES:]
    o_ref[...] = (_silu(gate) * val).astype(o_ref.dtype)


def _ffn_up(x, w_up, conv_w, conv_b, seq, *, tm=1024, tn=512, name):
    M, K = x.shape
    d_ff = w_up.shape[1] // 2
    tm = _tile(seq, tm, V7X_SUBLANES)
    tn = _tile(d_ff, tn)
    nn = d_ff // tn
    kc = conv_w.shape[0]
    conv_b = conv_b.reshape(1, 2 * d_ff)
    return pl.pallas_call(
        functools.partial(_ffn_up_kernel, tiles_per_seq=seq // tm),
        out_shape=jax.ShapeDtypeStruct((M, d_ff), BF16),
        grid=(nn, M // tm),
        in_specs=[pl.BlockSpec((tm, K), lambda n, m: (m, 0)),
                  pl.BlockSpec((K, tn), lambda n, m: (0, n)),
                  pl.BlockSpec((K, tn), lambda n, m: (0, n + nn)),
                  pl.BlockSpec((kc, tn), lambda n, m: (0, n)),
                  pl.BlockSpec((kc, tn), lambda n, m: (0, n + nn)),
                  pl.BlockSpec((1, tn), lambda n, m: (0, n)),
                  pl.BlockSpec((1, tn), lambda n, m: (0, n + nn))],
        out_specs=pl.BlockSpec((tm, tn), lambda n, m: (m, n)),
        scratch_shapes=[pltpu.VMEM((V7X_SUBLANES, tn), F32),
                        pltpu.VMEM((V7X_SUBLANES, tn), F32)],
        compiler_params=_params(("parallel", "arbitrary")),
        name=name,
    )(x, w_up, w_up, conv_w, conv_w, conv_b, conv_b)


def _ssd_kernel(z_ref, xs_ref, bc_ref, dt_ref, cwx_ref, cwbc_ref, cbx_ref, cbbc_ref,
                dtb_ref, alog_ref, dskip_ref, nw_ref, expand_ref, o_ref,
                tailx_ref, tailbc_ref, state_ref, acum_t_ref, y_ref, *, heads_per_group):
    c = pl.program_id(1)
    L = SSD_CHUNK
    P = SSM_HEAD_DIM
    N = D_STATE
    G = N_SSM_GROUPS
    R = heads_per_group
    H = G * R
    GP = R * P

    @pl.when(c == 0)
    def _():
        tailx_ref[...] = jnp.zeros_like(tailx_ref)
        tailbc_ref[...] = jnp.zeros_like(tailbc_ref)
        state_ref[...] = jnp.zeros_like(state_ref)

    x_raw = xs_ref[...].astype(F32)
    bc_raw = bc_ref[...].astype(F32)
    xs = _silu(_causal_dwconv(x_raw, tailx_ref[...], cwx_ref, cbx_ref))
    bc = _silu(_causal_dwconv(bc_raw, tailbc_ref[...], cwbc_ref, cbbc_ref))
    tailx_ref[...] = x_raw[-V7X_SUBLANES:]
    tailbc_ref[...] = bc_raw[-V7X_SUBLANES:]

    dt = jax.nn.softplus(dt_ref[...] + dtb_ref[...])
    adt = dt * (-jnp.exp(alog_ref[...]))

    row = lax.broadcasted_iota(jnp.int32, (L, L), 0)
    col = lax.broadcasted_iota(jnp.int32, (L, L), 1)
    causal = row >= col
    tril = jnp.where(causal, 1.0, 0.0).astype(BF16)
    a_hi, a_mid, a_lo = _split3(adt)
    acum = _dot(tril, a_hi) + _dot(tril, a_mid) + _dot(tril, a_lo)

    eye = jnp.where(lax.broadcasted_iota(jnp.int32, (H, H), 0)
                    == lax.broadcasted_iota(jnp.int32, (H, H), 1), 1.0, 0.0).astype(BF16)
    c_hi, c_mid, c_lo = _split3(acum)
    acum_t_ref[...] = _dot_nt(eye, c_hi) + _dot_nt(eye, c_mid) + _dot_nt(eye, c_lo)

    acum_last = acum[L - 1:L, :]
    decay_to_end = jnp.exp(acum_last - acum)
    decay_from_start = jnp.exp(acum)
    chunk_decay = jnp.broadcast_to(jnp.exp(acum_last), (V7X_SUBLANES, H))

    stacked = jnp.concatenate([dt, decay_to_end, decay_from_start, chunk_decay], axis=0)
    s_hi, s_mid, s_lo = _split3(stacked)
    expand = expand_ref[...]
    wide = _dot(s_hi, expand) + _dot(s_mid, expand) + _dot(s_lo, expand)
    dt_w = wide[0:L]
    dte_w = wide[L:2 * L]
    dfs_w = wide[2 * L:3 * L]
    cdecay_w = wide[3 * L:3 * L + 1]

    xdt = xs * dt_w
    xdt_bf = xdt.astype(BF16)
    xend_bf = (xdt * dte_w).astype(BF16)

    for g in range(G):
        gs = slice(g * GP, (g + 1) * GP)
        b_g = bc[:, g * N:(g + 1) * N]
        c_g = bc[:, G * N + g * N:G * N + (g + 1) * N].astype(BF16)
        b_gt = b_g.T.astype(BF16)
        cb = _dot(c_g, b_gt)
        state_g = state_ref[:, gs]
        y_off = _dot(c_g, state_g.astype(BF16)) * dfs_w[:, gs]
        state_ref[:, gs] = state_g * cdecay_w[:, gs] + _dot(b_gt, xend_bf[:, gs])
        for r in range(R):
            h = g * R + r
            seg = acum[:, h:h + 1] - acum_t_ref[h:h + 1, :]
            m = cb * jnp.exp(jnp.where(causal, seg, -jnp.inf))
            y_ref[:, h * P:(h + 1) * P] = _dot(m.astype(BF16), xdt_bf[:, h * P:(h + 1) * P])
        y = y_ref[:, gs] + y_off + dskip_ref[:, gs] * xs[:, gs]
        y = y * _silu(z_ref[:, gs].astype(F32))
        ms = jnp.mean(y * y, axis=-1, keepdims=True)
        o_ref[:, gs] = (y * lax.rsqrt(ms + RMS_EPS) * nw_ref[:, gs]).astype(o_ref.dtype)


def _ssd(zx, dt_raw, conv_w, conv_b, dt_bias, a_log, d_skip, norm_w, batch, seq, *, name):
    M = zx.shape[0]
    d_inner = norm_w.shape[0]
    H = dt_bias.shape[0]
    R = H // N_SSM_GROUPS
    gn2 = 2 * N_SSM_GROUPS * D_STATE
    assert d_inner == H * SSM_HEAD_DIM and d_inner % gn2 == 0 and seq % SSD_CHUNK == 0
    L = SSD_CHUNK
    nc = seq // L
    kc = conv_w.shape[0]
    conv_b = conv_b.reshape(1, -1)
    expand = jnp.repeat(jnp.eye(H, dtype=BF16), SSM_HEAD_DIM, axis=1)
    d_wide = jnp.repeat(d_skip.astype(F32), SSM_HEAD_DIM).reshape(1, d_inner)
    bc_blk = 2 * d_inner // gn2
    row = lambda b, c: b * nc + c
    const = lambda b, c: (0, 0)
    return pl.pallas_call(
        functools.partial(_ssd_kernel, heads_per_group=R),
        out_shape=jax.ShapeDtypeStruct((M, d_inner), BF16),
        grid=(batch, nc),
        in_specs=[pl.BlockSpec((L, d_inner), lambda b, c: (row(b, c), 0)),
                  pl.BlockSpec((L, d_inner), lambda b, c: (row(b, c), 1)),
                  pl.BlockSpec((L, gn2), lambda b, c: (row(b, c), bc_blk)),
                  pl.BlockSpec((L, H), lambda b, c: (row(b, c), 0)),
                  pl.BlockSpec((kc, d_inner), const),
                  pl.BlockSpec((kc, gn2), lambda b, c: (0, d_inner // gn2)),
                  pl.BlockSpec((1, d_inner), const),
                  pl.BlockSpec((1, gn2), lambda b, c: (0, d_inner // gn2)),
                  pl.BlockSpec((1, H), const),
                  pl.BlockSpec((1, H), const),
                  pl.BlockSpec((1, d_inner), const),
                  pl.BlockSpec((1, d_inner), const),
                  pl.BlockSpec((H, d_inner), const)],
        out_specs=pl.BlockSpec((L, d_inner), lambda b, c: (row(b, c), 0)),
        scratch_shapes=[pltpu.VMEM((V7X_SUBLANES, d_inner), F32),
                        pltpu.VMEM((V7X_SUBLANES, gn2), F32),
                        pltpu.VMEM((D_STATE, d_inner), F32),
                        pltpu.VMEM((H, L), F32),
                        pltpu.VMEM((L, d_inner), F32)],
        compiler_params=_params(("parallel", "arbitrary")),
        name=name,
    )(zx, zx, zx, dt_raw, conv_w, conv_w, conv_b, conv_b,
      dt_bias.reshape(1, H).astype(F32), a_log.reshape(1, H).astype(F32), d_wide,
      norm_w.reshape(1, d_inner).astype(F32), expand)


def _k_proj_kernel(x_ref, w_ref, k_ref, km_ref):
    acc = _dot(x_ref[...], w_ref[...])
    k_ref[...] = acc.astype(k_ref.dtype)
    nb, tn = km_ref.shape[1], km_ref.shape[2]
    km_ref[0] = jnp.mean(acc.reshape(nb, MOBA_BLOCK, tn), axis=1)


def _k_proj(x, w, *, tm=1024, tn=1024, name):
    M, K = x.shape
    N = w.shape[1]
    tm = _tile(M, tm, MOBA_BLOCK)
    tn = _tile(N, tn)
    nb = tm // MOBA_BLOCK
    k, km = pl.pallas_call(
        _k_proj_kernel,
        out_shape=(jax.ShapeDtypeStruct((M, N), BF16),
                   jax.ShapeDtypeStruct((M // tm, nb, N), F32)),
        grid=(M // tm, N // tn),
        in_specs=[pl.BlockSpec((tm, K), lambda m, n: (m, 0)),
                  pl.BlockSpec((K, tn), lambda m, n: (0, n))],
        out_specs=(pl.BlockSpec((tm, tn), lambda m, n: (m, n)),
                   pl.BlockSpec((1, nb, tn), lambda m, n: (m, 0, n))),
        compiler_params=_params(("parallel", "arbitrary")),
        name=name,
    )(x, w)
    return k, km.reshape(M // MOBA_BLOCK, N)


MOBA_MASK_VALUE = -1e30
MOBA_FEATURES = 128
MOBA_GROUP = 4


def _moba_key_features(seq):
    nb = seq // MOBA_BLOCK
    pos = jnp.arange(seq)
    blk = pos // MOBA_BLOCK
    onehot = (blk[:, None] == jnp.arange(nb)[None, :]).astype(F32)
    coarse = jnp.broadcast_to((blk * MOBA_BLOCK).astype(F32)[:, None], (seq, 3))
    fine = jnp.broadcast_to((pos % MOBA_BLOCK).astype(F32)[:, None], (seq, 3))
    zeros = jnp.zeros((seq, MOBA_FEATURES - nb - 6), F32)
    return jnp.concatenate([onehot, coarse, fine, zeros], axis=1).astype(BF16)


def _moba_slope_features(n_heads):
    slope = 2.0 ** (-8.0 * jnp.arange(1, n_heads + 1, dtype=F32) / n_heads) / (HEAD_DIM ** -0.5)
    hi = slope.astype(BF16).astype(F32)
    mid = (slope - hi).astype(BF16).astype(F32)
    lo = (slope - hi - mid).astype(BF16).astype(F32)
    rows = jnp.stack([hi, mid, lo, hi, mid, lo, slope, jnp.zeros_like(slope)], axis=1)
    return jnp.broadcast_to(rows[:, :, None], (n_heads, 8, MOBA_BLOCK))


def _fold_max(s):
    return jnp.max(s.reshape(s.shape[0] // V7X_SUBLANES, V7X_SUBLANES, s.shape[1]), axis=0)


def _moba_kernel(qt_ref, k_ref, vt_ref, km_ref, kfeat_ref, sfeat_ref, o_ref,
                 s_ref, acc_ref, *, n_blocks):
    i = pl.program_id(2)
    T = MOBA_BLOCK
    GT = MOBA_GROUP * T
    DH = HEAD_DIM
    NB = n_blocks
    qt = qt_ref[...]
    sfeat = sfeat_ref[0]
    log2e_scale = (HEAD_DIM ** -0.5) * math.log2(math.e)

    gate = _dot(km_ref[...].astype(BF16), qt)
    blk = lax.broadcasted_iota(jnp.int32, gate.shape, 0).astype(F32)
    valid = blk < i.astype(F32)
    neg_inf = jnp.float32(-jnp.inf)
    chosen = jnp.zeros(gate.shape, dtype=jnp.bool_)
    for _ in range(min(MOBA_TOPK, NB)):
        cur = jnp.where(jnp.logical_and(valid, jnp.logical_not(chosen)), gate, neg_inf)
        best = jnp.max(cur, axis=0, keepdims=True)
        hit = jnp.logical_and(cur == best, cur > neg_inf)
        first = jnp.min(jnp.where(hit, blk, float(NB)), axis=0, keepdims=True)
        chosen = jnp.logical_or(chosen, blk == first)
    mask_rows = jnp.where(chosen, 0.0, MOBA_MASK_VALUE)
    feat = jnp.concatenate(
        [mask_rows, sfeat, jnp.zeros((MOBA_FEATURES - NB - 8, T), F32)], axis=0)
    q_aug = jnp.concatenate([qt, feat.astype(BF16)], axis=0)

    def ones_row(width):
        first = lax.broadcasted_iota(jnp.int32, (DH, width), 0) == 0
        return jnp.where(first, 1.0, 0.0).astype(BF16)

    own = pl.ds(pl.multiple_of(i * T, T), T)
    key_idx = lax.broadcasted_iota(jnp.int32, (T, T), 0)
    qry_idx = lax.broadcasted_iota(jnp.int32, (T, T), 1)
    key_pos = (key_idx + i * T).astype(F32)
    s_own = _dot(k_ref[own, :], qt) + sfeat_ref[0, 6:7, :] * key_pos
    s_own = jnp.where(key_idx <= qry_idx, s_own, MOBA_MASK_VALUE)
    s_ref[pl.ds(NB * T, T), :] = s_own

    n_groups = (i + MOBA_GROUP - 1) // MOBA_GROUP

    def scores(g, mx):
        rows = pl.ds(pl.multiple_of(g * GT, GT), GT)
        k_aug = jnp.concatenate([k_ref[rows, :], kfeat_ref[rows, :]], axis=1)
        s = _dot(k_aug, q_aug)
        s_ref[rows, :] = s
        return jnp.maximum(mx, _fold_max(s))

    mx = lax.fori_loop(0, n_groups, scores, _fold_max(s_own))
    m_scaled = jnp.max(mx, axis=0, keepdims=True) * log2e_scale

    p_own = jnp.exp2(s_ref[pl.ds(NB * T, T), :] * log2e_scale - m_scaled).astype(BF16)
    vt_own = jnp.concatenate([vt_ref[:, own], ones_row(T)], axis=0)
    acc_ref[...] = _dot(vt_own, p_own)

    def weighted(g, carry):
        rows = pl.ds(pl.multiple_of(g * GT, GT), GT)
        p = jnp.exp2(s_ref[rows, :] * log2e_scale - m_scaled).astype(BF16)
        vt_aug = jnp.concatenate([vt_ref[:, rows], ones_row(GT)], axis=0)
        acc_ref[...] += _dot(vt_aug, p)
        return carry

    lax.fori_loop(0, n_groups, weighted, 0)
    out_t = acc_ref[:DH, :] / acc_ref[DH:DH + 1, :]
    o_ref[...] = out_t.T.astype(o_ref.dtype)


def _moba(qt, k, vt, kmean, batch, seq, *, name):
    D = k.shape[1]
    n_heads = D // HEAD_DIM
    nb = seq // MOBA_BLOCK
    assert nb % MOBA_GROUP == 0 and nb + 8 <= MOBA_FEATURES and nb % V7X_SUBLANES == 0
    T = MOBA_BLOCK
    return pl.pallas_call(
        functools.partial(_moba_kernel, n_blocks=nb),
        out_shape=jax.ShapeDtypeStruct((batch * seq, D), BF16),
        grid=(batch, n_heads, nb),
        in_specs=[pl.BlockSpec((HEAD_DIM, T), lambda b, h, i: (h, b * nb + i)),
                  pl.BlockSpec((seq, HEAD_DIM), lambda b, h, i: (b, h)),
                  pl.BlockSpec((HEAD_DIM, seq), lambda b, h, i: (h, b)),
                  pl.BlockSpec((nb, HEAD_DIM), lambda b, h, i: (b, h)),
                  pl.BlockSpec((seq, MOBA_FEATURES), lambda b, h, i: (0, 0)),
                  pl.BlockSpec((1, 8, T), lambda b, h, i: (h, 0, 0))],
        out_specs=pl.BlockSpec((T, HEAD_DIM), lambda b, h, i: (b * nb + i, h)),
        scratch_shapes=[pltpu.VMEM(((nb + 1) * T, T), F32),
                        pltpu.VMEM((2 * HEAD_DIM, T), F32)],
        compiler_params=_params(("parallel", "parallel", "arbitrary")),
        name=name,
    )(qt, k, vt, kmean, _moba_key_features(seq), _moba_slope_features(n_heads))


def _conv_ffn_block(h_f32, h_bf, w_up, conv_w, conv_b, w_down, g, b, alpha, seq, tag):
    act = _ffn_up(h_bf, w_up.astype(BF16), conv_w, conv_b, seq, name=f"ffn_up_{tag}")
    return _matmul_res_ln(act, w_down.astype(BF16), h_f32, g, b, alpha, tk=1408,
                          name=f"ffn_down_ln_{tag}")


def kernel(x, mamba_w_in, mamba_conv_w, mamba_conv_b, mamba_dt_bias, mamba_a_log, mamba_d,
           mamba_norm_w, mamba_w_out, ffn_w_up, ffn_conv_w, ffn_conv_b, ffn_w_down,
           ln_mix_g, ln_mix_b, ln_ffn_g, ln_ffn_b, w_kv, attn_w_q, attn_w_o):
    batch, seq, d_model = x.shape
    depth = ffn_w_up.shape[0]
    n_a = mamba_w_in.shape[0]
    alpha = (2.0 * depth) ** 0.25
    M = batch * seq

    h = x.reshape(M, d_model)
    h_bf = h.astype(BF16)
    k = v = kmean = None
    for layer in range(depth):
        if layer < n_a:
            d_inner = mamba_norm_w.shape[1]
            n_zx = mamba_w_in.shape[2] - mamba_dt_bias.shape[1]
            w_in = mamba_w_in[layer]
            zx = _matmul(h_bf, w_in[:, :n_zx].astype(BF16), F32, name=f"in_proj_{layer}")
            dt_raw = _matmul(h_bf, w_in[:, n_zx:].astype(BF16), F32, name=f"dt_proj_{layer}")
            y = _ssd(zx, dt_raw, mamba_conv_w[layer], mamba_conv_b[layer], mamba_dt_bias[layer],
                     mamba_a_log[layer], mamba_d[layer], mamba_norm_w[layer], batch, seq,
                     name=f"ssd_{layer}")
            mix_in, w_mix = y, mamba_w_out[layer]
        else:
            j = layer - n_a
            q = _matmul(h_bf, attn_w_q[j].astype(BF16), BF16, name=f"q_proj_{j}")
            o = _moba(q.T, k, v.T, kmean, batch, seq, name=f"moba_{j}")
            mix_in, w_mix = o, attn_w_o[j]
        h, h_bf = _matmul_res_ln(mix_in, w_mix.astype(BF16), h, ln_mix_g[layer], ln_mix_b[layer],
                                 alpha, tk=2048, name=f"mix_out_ln_{layer}")
        h, h_bf = _conv_ffn_block(h, h_bf, ffn_w_up[layer], ffn_conv_w[layer], ffn_conv_b[layer],
                                  ffn_w_down[layer], ln_ffn_g[layer], ln_ffn_b[layer], alpha, seq,
                                  layer)
        if layer == n_a - 1:
            k, kmean = _k_proj(h_bf, w_kv[:, :d_model].astype(BF16), name="k_proj")
            v = _matmul(h_bf, w_kv[:, d_model:].astype(BF16), BF16, name="v_proj")
    return h.reshape(batch, seq, d_model)
```

```python
import functools
import math

import jax
import jax.numpy as jnp
from jax import lax
from jax.experimental import pallas as pl
from jax.experimental.pallas import tpu as pltpu

F32 = jnp.float32
BF16 = jnp.bfloat16

SSM_HEAD_DIM = 64
D_STATE = 128
N_SSM_GROUPS = 8
SSD_CHUNK = 128
HEAD_DIM = 128
MOBA_BLOCK = 256
MOBA_TOPK = 3
LN_EPS = 1e-5
RMS_EPS = 1e-5

V7X_LANES = 128
V7X_SUBLANES = 8
V7X_VMEM_LIMIT_BYTES = 56 * 1024 * 1024


def _tile(dim, target, unit=V7X_LANES):
    if dim <= target:
        return dim
    best = None
    t = unit
    while t <= target:
        if dim % t == 0:
            best = t
        t += unit
    assert best is not None, (dim, target, unit)
    return best


def _params(semantics):
    return pltpu.CompilerParams(dimension_semantics=semantics,
                                vmem_limit_bytes=V7X_VMEM_LIMIT_BYTES)


def _dot(a, b):
    return jnp.dot(a, b, preferred_element_type=F32)


def _dot_nt(a, b):
    return lax.dot_general(a, b, (((1,), (1,)), ((), ())), preferred_element_type=F32)


def _split3(v):
    hi = v.astype(BF16)
    r1 = v - hi.astype(F32)
    mid = r1.astype(BF16)
    lo = (r1 - mid.astype(F32)).astype(BF16)
    return hi, mid, lo


def _silu(v):
    return v * jax.nn.sigmoid(v)


def _mm_kernel(x_ref, w_ref, o_ref, *, silu):
    acc = _dot(x_ref[...], w_ref[...])
    if silu:
        acc = _silu(acc)
    o_ref[...] = acc.astype(o_ref.dtype)


def _matmul(x, w, out_dtype, *, silu=False, tm=1024, tn=1024, name):
    M, K = x.shape
    N = w.shape[1]
    tm = _tile(M, tm)
    tn = _tile(N, tn)
    return pl.pallas_call(
        functools.partial(_mm_kernel, silu=silu),
        out_shape=jax.ShapeDtypeStruct((M, N), out_dtype),
        grid=(M // tm, N // tn),
        in_specs=[pl.BlockSpec((tm, K), lambda m, n: (m, 0)),
                  pl.BlockSpec((K, tn), lambda m, n: (0, n))],
        out_specs=pl.BlockSpec((tm, tn), lambda m, n: (m, n)),
        compiler_params=_params(("parallel", "arbitrary")),
        name=name,
    )(x, w)


def _mm_ln_kernel(x_ref, w_ref, h_ref, g_ref, b_ref, of_ref, ob_ref, *, alpha):
    y = alpha * h_ref[...] + _dot(x_ref[...], w_ref[...])
    mu = jnp.mean(y, axis=-1, keepdims=True)
    d = y - mu
    var = jnp.mean(d * d, axis=-1, keepdims=True)
    out = d * lax.rsqrt(var + LN_EPS) * g_ref[...] + b_ref[...]
    of_ref[...] = out
    ob_ref[...] = out.astype(BF16)


def _matmul_res_ln(x, w, h, g, b, alpha, *, tm, name):
    M, K = x.shape
    N = w.shape[1]
    tm = _tile(M, tm)
    return pl.pallas_call(
        functools.partial(_mm_ln_kernel, alpha=alpha),
        out_shape=(jax.ShapeDtypeStruct((M, N), F32), jax.ShapeDtypeStruct((M, N), BF16)),
        grid=(M // tm,),
        in_specs=[pl.BlockSpec((tm, K), lambda m: (m, 0)),
                  pl.BlockSpec((K, N), lambda m: (0, 0), pipeline_mode=pl.Buffered(1)),
                  pl.BlockSpec((tm, N), lambda m: (m, 0)),
                  pl.BlockSpec((1, N), lambda m: (0, 0)),
                  pl.BlockSpec((1, N), lambda m: (0, 0))],
        out_specs=(pl.BlockSpec((tm, N), lambda m: (m, 0)),
                   pl.BlockSpec((tm, N), lambda m: (m, 0))),
        compiler_params=_params(("parallel",)),
        name=name,
    )(x, w, h, g.reshape(1, N), b.reshape(1, N))


CONV_COLS = 512


def _shift_rows(u, tail, k):
    tm, cw = u.shape
    groups = jnp.concatenate([tail, u], axis=0).reshape(tm // V7X_SUBLANES + 1, V7X_SUBLANES, cw)
    rolled = pltpu.roll(groups, k, axis=1)
    sub = lax.broadcasted_iota(jnp.int32, (1, V7X_SUBLANES, cw), 1)
    out = jnp.where(sub < k, rolled[:-1], rolled[1:])
    return out.reshape(tm, cw)


def _conv_rows(u, tail, w_ref, b_ref, cols):
    K = w_ref.shape[0]
    out = b_ref[:, cols] + w_ref[K - 1:K, cols] * u
    for k in range(1, K):
        out = out + w_ref[K - 1 - k:K - k, cols] * _shift_rows(u, tail, k)
    return out


def _mm_conv_silu_kernel(x_ref, w_ref, cw_ref, cb_ref, o_ref, tail_ref, *, tiles_per_seq):
    m = pl.program_id(1)
    tn = o_ref.shape[1]

    @pl.when(m % tiles_per_seq == 0)
    def _():
        tail_ref[...] = jnp.zeros_like(tail_ref)

    x = x_ref[...]
    chunks = [slice(c, c + CONV_COLS) for c in range(0, tn, CONV_COLS)]
    us = [_dot(x, w_ref[:, cols]) for cols in chunks]
    for cols, u in zip(chunks, us):
        out = _silu(_conv_rows(u, tail_ref[:, cols], cw_ref, cb_ref, cols))
        o_ref[:, cols] = out.astype(o_ref.dtype)
        tail_ref[:, cols] = u[-V7X_SUBLANES:]


def _matmul_conv_silu(x, w, conv_w, conv_b, seq, *, tm=1024, tn=1024, name):
    M, K = x.shape
    N = w.shape[1]
    tm = _tile(seq, tm, V7X_SUBLANES)
    tn = _tile(N, tn)
    kc = conv_w.shape[0]
    return pl.pallas_call(
        functools.partial(_mm_conv_silu_kernel, tiles_per_seq=seq // tm),
        out_shape=jax.ShapeDtypeStruct((M, N), BF16),
        grid=(N // tn, M // tm),
        in_specs=[pl.BlockSpec((tm, K), lambda n, m: (m, 0)),
                  pl.BlockSpec((K, tn), lambda n, m: (0, n)),
                  pl.BlockSpec((kc, tn), lambda n, m: (0, n)),
                  pl.BlockSpec((1, tn), lambda n, m: (0, n))],
        out_specs=pl.BlockSpec((tm, tn), lambda n, m: (m, n)),
        scratch_shapes=[pltpu.VMEM((V7X_SUBLANES, tn), F32)],
        compiler_params=_params(("parallel", "arbitrary")),
        name=name,
    )(x, w, conv_w, conv_b.reshape(1, N))


def _ffn_up_kernel(x_ref, wg_ref, wv_ref, cwg_ref, cwv_ref, cbg_ref, cbv_ref, o_ref,
                   tailg_ref, tailv_ref, *, tiles_per_seq):
    m = pl.program_id(1)

    @pl.when(m % tiles_per_seq == 0)
    def _():
        tailg_ref[...] = jnp.zeros_like(tailg_ref)
        tailv_ref[...] = jnp.zeros_like(tailv_ref)

    x = x_ref[...]
    ug = _dot(x, wg_ref[...])
    uv = _dot(x, wv_ref[...])
    cols = slice(None)
    gate = _conv_rows(ug, tailg_ref[...], cwg_ref, cbg_ref, cols)
    val = _conv_rows(uv, tailv_ref[...], cwv_ref, cbv_ref, cols)
    o_ref[...] = (_silu(gate) * val).astype(o_ref.dtype)
    tailg_ref[...] = ug[-V7X_SUBLANES:]
    tailv_ref[...] = uv[-V7X_SUBLANES:]


def _ffn_up(x, w_up, conv_w, conv_b, seq, *, tm=1024, tn=512, name):
    M, K = x.shape
    d_ff = w_up.shape[1] // 2
    tm = _tile(seq, tm, V7X_SUBLANES)
    tn = _tile(d_ff, tn)
    nn = d_ff // tn
    kc = conv_w.shape[0]
    conv_b = conv_b.reshape(1, 2 * d_ff)
    return pl.pallas_call(
        functools.partial(_ffn_up_kernel, tiles_per_seq=seq // tm),
        out_shape=jax.ShapeDtypeStruct((M, d_ff), BF16),
        grid=(nn, M // tm),
        in_specs=[pl.BlockSpec((tm, K), lambda n, m: (m, 0)),
                  pl.BlockSpec((K, tn), lambda n, m: (0, n)),
                  pl.BlockSpec((K, tn), lambda n, m: (0, n + nn)),
                  pl.BlockSpec((kc, tn), lambda n, m: (0, n)),
                  pl.BlockSpec((kc, tn), lambda n, m: (0, n + nn)),
                  pl.BlockSpec((1, tn), lambda n, m: (0, n)),
                  pl.BlockSpec((1, tn), lambda n, m: (0, n + nn))],
        out_specs=pl.BlockSpec((tm, tn), lambda n, m: (m, n)),
        scratch_shapes=[pltpu.VMEM((V7X_SUBLANES, tn), F32),
                        pltpu.VMEM((V7X_SUBLANES, tn), F32)],
        compiler_params=_params(("parallel", "arbitrary")),
        name=name,
    )(x, w_up, w_up, conv_w, conv_w, conv_b, conv_b)


def _ssd_kernel(z_ref, xs_ref, bc_ref, dt_ref, dtb_ref, alog_ref, dskip_ref, nw_ref, expand_ref,
                o_ref, state_ref, acum_t_ref, y_ref, *, heads_per_group):
    c = pl.program_id(1)
    L = SSD_CHUNK
    P = SSM_HEAD_DIM
    N = D_STATE
    G = N_SSM_GROUPS
    R = heads_per_group
    H = G * R
    GP = R * P

    @pl.when(c == 0)
    def _():
        state_ref[...] = jnp.zeros_like(state_ref)

    dt = jax.nn.softplus(dt_ref[...] + dtb_ref[...])
    adt = dt * (-jnp.exp(alog_ref[...]))

    row = lax.broadcasted_iota(jnp.int32, (L, L), 0)
    col = lax.broadcasted_iota(jnp.int32, (L, L), 1)
    causal = row >= col
    tril = jnp.where(causal, 1.0, 0.0).astype(BF16)
    acum = _dot(jnp.concatenate([tril, tril, tril], axis=1),
                jnp.concatenate(_split3(adt), axis=0))

    eye = jnp.where(lax.broadcasted_iota(jnp.int32, (H, H), 0)
                    == lax.broadcasted_iota(jnp.int32, (H, H), 1), 1.0, 0.0).astype(BF16)
    acum_t_ref[...] = _dot_nt(jnp.concatenate([eye, eye, eye], axis=1),
                              jnp.concatenate(_split3(acum), axis=1))

    acum_last = acum[L - 1:L, :]
    decay_to_end = jnp.exp(acum_last - acum)
    decay_from_start = jnp.exp(acum)
    chunk_decay = jnp.broadcast_to(jnp.exp(acum_last), (V7X_SUBLANES, H))

    stacked = jnp.concatenate([dt, decay_to_end, decay_from_start, chunk_decay], axis=0)
    wide = _dot(jnp.concatenate(_split3(stacked), axis=1), expand_ref[...])
    dt_w = wide[0:L]
    dte_w = wide[L:2 * L]
    dfs_w = wide[2 * L:3 * L]
    cdecay_w = wide[3 * L:3 * L + 1]

    xs = xs_ref[...].astype(F32)
    xdt = xs * dt_w
    xdt_bf = xdt.astype(BF16)
    xend_bf = (xdt * dte_w).astype(BF16)
    low_half = lax.broadcasted_iota(jnp.int32, (L, 2 * P), 1) < P

    for g in range(G):
        gs = slice(g * GP, (g + 1) * GP)
        c_g = bc_ref[:, G * N + g * N:G * N + (g + 1) * N]
        b_gt = bc_ref[:, g * N:(g + 1) * N].astype(F32).T.astype(BF16)
        cb = _dot(c_g, b_gt)
        state_g = state_ref[:, gs]
        y_off = _dot(c_g, state_g.astype(BF16)) * dfs_w[:, gs]
        state_ref[:, gs] = state_g * cdecay_w[:, gs] + _dot(b_gt, xend_bf[:, gs])
        for pair in range(R // 2):
            h0 = g * R + 2 * pair
            m_pair = []
            for h in (h0, h0 + 1):
                seg = acum[:, h:h + 1] - acum_t_ref[h:h + 1, :]
                m_pair.append((cb * jnp.exp(jnp.where(causal, seg, -jnp.inf))).astype(BF16))
            x_pair = xdt_bf[:, h0 * P:(h0 + 2) * P]
            zero = jnp.zeros_like(x_pair)
            rhs = jnp.concatenate([jnp.where(low_half, x_pair, zero),
                                   jnp.where(low_half, zero, x_pair)], axis=0)
            y_ref[:, h0 * P:(h0 + 2) * P] = _dot(jnp.concatenate(m_pair, axis=1), rhs)
        y = y_ref[:, gs] + y_off + dskip_ref[:, gs] * xs[:, gs]
        y = y * z_ref[:, gs].astype(F32)
        ms = jnp.mean(y * y, axis=-1, keepdims=True)
        o_ref[:, gs] = (y * lax.rsqrt(ms + RMS_EPS) * nw_ref[:, gs]).astype(o_ref.dtype)


def _ssd(z_act, xbc_act, dt_raw, dt_bias, a_log, d_skip, norm_w, batch, seq, *, name):
    M, d_inner = z_act.shape
    H = dt_bias.shape[0]
    R = H // N_SSM_GROUPS
    gn2 = 2 * N_SSM_GROUPS * D_STATE
    assert d_inner == H * SSM_HEAD_DIM and d_inner % gn2 == 0 and seq % SSD_CHUNK == 0
    assert R % 2 == 0
    L = SSD_CHUNK
    nc = seq // L
    expand = jnp.repeat(jnp.eye(H, dtype=BF16), SSM_HEAD_DIM, axis=1)
    expand3 = jnp.concatenate([expand, expand, expand], axis=0)
    d_wide = jnp.repeat(d_skip.astype(F32), SSM_HEAD_DIM).reshape(1, d_inner)
    row = lambda b, c: b * nc + c
    const = lambda b, c: (0, 0)
    return pl.pallas_call(
        functools.partial(_ssd_kernel, heads_per_group=R),
        out_shape=jax.ShapeDtypeStruct((M, d_inner), BF16),
        grid=(batch, nc),
        in_specs=[pl.BlockSpec((L, d_inner), lambda b, c: (row(b, c), 0)),
                  pl.BlockSpec((L, d_inner), lambda b, c: (row(b, c), 0)),
                  pl.BlockSpec((L, gn2), lambda b, c: (row(b, c), d_inner // gn2)),
                  pl.BlockSpec((L, H), lambda b, c: (row(b, c), 0)),
                  pl.BlockSpec((1, H), const),
                  pl.BlockSpec((1, H), const),
                  pl.BlockSpec((1, d_inner), const),
                  pl.BlockSpec((1, d_inner), const),
                  pl.BlockSpec((3 * H, d_inner), const)],
        out_specs=pl.BlockSpec((L, d_inner), lambda b, c: (row(b, c), 0)),
        scratch_shapes=[pltpu.VMEM((D_STATE, d_inner), F32),
                        pltpu.VMEM((H, L), F32),
                        pltpu.VMEM((L, d_inner), F32)],
        compiler_params=_params(("parallel", "arbitrary")),
        name=name,
    )(z_act, xbc_act, xbc_act, dt_raw,
      dt_bias.reshape(1, H).astype(F32), a_log.reshape(1, H).astype(F32), d_wide,
      norm_w.reshape(1, d_inner).astype(F32), expand3)


def _k_proj_kernel(x_ref, w_ref, k_ref, km_ref):
    acc = _dot(x_ref[...], w_ref[...])
    k_ref[...] = acc.astype(k_ref.dtype)
    nb, tn = km_ref.shape[1], km_ref.shape[2]
    km_ref[0] = jnp.mean(acc.reshape(nb, MOBA_BLOCK, tn), axis=1)


def _k_proj(x, w, *, tm=1024, tn=1024, name):
    M, K = x.shape
    N = w.shape[1]
    tm = _tile(M, tm, MOBA_BLOCK)
    tn = _tile(N, tn)
    nb = tm // MOBA_BLOCK
    k, km = pl.pallas_call(
        _k_proj_kernel,
        out_shape=(jax.ShapeDtypeStruct((M, N), BF16),
                   jax.ShapeDtypeStruct((M // tm, nb, N), F32)),
        grid=(M // tm, N // tn),
        in_specs=[pl.BlockSpec((tm, K), lambda m, n: (m, 0)),
                  pl.BlockSpec((K, tn), lambda m, n: (0, n))],
        out_specs=(pl.BlockSpec((tm, tn), lambda m, n: (m, n)),
                   pl.BlockSpec((1, nb, tn), lambda m, n: (m, 0, n))),
        compiler_params=_params(("parallel", "arbitrary")),
        name=name,
    )(x, w)
    return k, km.reshape(M // MOBA_BLOCK, N)


MOBA_MASK_VALUE = -1e30
MOBA_FEATURES = 128
MOBA_GROUP = 4
MOBA_HEADS = 2


def _moba_key_features(seq):
    nb = seq // MOBA_BLOCK
    pos = jnp.arange(seq)
    blk = pos // MOBA_BLOCK
    onehot = (blk[:, None] == jnp.arange(nb)[None, :]).astype(F32)
    coarse = jnp.broadcast_to((blk * MOBA_BLOCK).astype(F32)[:, None], (seq, 3))
    fine = jnp.broadcast_to((pos % MOBA_BLOCK).astype(F32)[:, None], (seq, 3))
    zeros = jnp.zeros((seq, MOBA_FEATURES - nb - 6), F32)
    return jnp.concatenate([onehot, coarse, fine, zeros], axis=1).astype(BF16)


def _moba_slope_features(n_heads):
    slope = 2.0 ** (-8.0 * jnp.arange(1, n_heads + 1, dtype=F32) / n_heads) / (HEAD_DIM ** -0.5)
    hi = slope.astype(BF16).astype(F32)
    mid = (slope - hi).astype(BF16).astype(F32)
    lo = (slope - hi - mid).astype(BF16).astype(F32)
    rows = jnp.stack([hi, mid, lo, hi, mid, lo, slope, jnp.zeros_like(slope)], axis=1)
    return jnp.broadcast_to(rows[:, :, None], (n_heads, 8, MOBA_BLOCK))


def _fold_max(s):
    return jnp.max(s.reshape(s.shape[0] // V7X_SUBLANES, V7X_SUBLANES, s.shape[1]), axis=0)


def _moba_kernel(qt_ref, k_ref, vt_ref, km_ref, kfeat_ref, sfeat_ref, o_ref,
                 s_ref, acc_ref, *, n_blocks):
    i = pl.program_id(2)
    T = MOBA_BLOCK
    GT = MOBA_GROUP * T
    DH = HEAD_DIM
    NB = n_blocks
    heads = range(MOBA_HEADS)
    log2e_scale = (HEAD_DIM ** -0.5) * math.log2(math.e)
    own = pl.ds(pl.multiple_of(i * T, T), T)
    own_slot = pl.ds(NB * T, T)
    n_groups = (i + MOBA_GROUP - 1) // MOBA_GROUP

    def head_cols(hh):
        return slice(hh * DH, (hh + 1) * DH)

    def ones_row(width):
        first = lax.broadcasted_iota(jnp.int32, (DH, width), 0) == 0
        return jnp.where(first, 1.0, 0.0).astype(BF16)

    def select(hh):
        qt = qt_ref[head_cols(hh), :]
        gate = _dot(km_ref[:, head_cols(hh)].astype(BF16), qt)
        blk = lax.broadcasted_iota(jnp.int32, gate.shape, 0).astype(F32)
        valid = blk < i.astype(F32)
        neg_inf = jnp.float32(-jnp.inf)
        chosen = jnp.zeros(gate.shape, dtype=jnp.bool_)
        for _ in range(min(MOBA_TOPK, NB)):
            cur = jnp.where(jnp.logical_and(valid, jnp.logical_not(chosen)), gate, neg_inf)
            best = jnp.max(cur, axis=0, keepdims=True)
            hit = jnp.logical_and(cur == best, cur > neg_inf)
            first = jnp.min(jnp.where(hit, blk, float(NB)), axis=0, keepdims=True)
            chosen = jnp.logical_or(chosen, blk == first)
        mask_rows = jnp.where(chosen, 0.0, MOBA_MASK_VALUE)
        feat = jnp.concatenate(
            [mask_rows, sfeat_ref[hh], jnp.zeros((MOBA_FEATURES - NB - 8, T), F32)], axis=0)
        return jnp.concatenate([qt, feat.astype(BF16)], axis=0)

    q_aug = [select(hh) for hh in heads]

    key_idx = lax.broadcasted_iota(jnp.int32, (T, T), 0)
    qry_idx = lax.broadcasted_iota(jnp.int32, (T, T), 1)
    key_pos = (key_idx + i * T).astype(F32)
    mx0 = []
    for hh in heads:
        s_own = _dot(k_ref[own, head_cols(hh)], qt_ref[head_cols(hh), :])
        s_own = s_own + sfeat_ref[hh, 6:7, :] * key_pos
        s_own = jnp.where(key_idx <= qry_idx, s_own, MOBA_MASK_VALUE)
        s_ref[hh, own_slot, :] = s_own
        mx0.append(_fold_max(s_own))

    def scores(g, mx):
        rows = pl.ds(pl.multiple_of(g * GT, GT), GT)
        kfeat = kfeat_ref[rows, :]
        new = []
        for hh in heads:
            k_aug = jnp.concatenate([k_ref[rows, head_cols(hh)], kfeat], axis=1)
            s = _dot(k_aug, q_aug[hh])
            s_ref[hh, rows, :] = s
            new.append(jnp.maximum(mx[hh], _fold_max(s)))
        return tuple(new)

    mx = lax.fori_loop(0, n_groups, scores, tuple(mx0))
    m_scaled = [jnp.max(mx[hh], axis=0, keepdims=True) * log2e_scale for hh in heads]

    for hh in heads:
        p_own = jnp.exp2(s_ref[hh, own_slot, :] * log2e_scale - m_scaled[hh]).astype(BF16)
        vt_own = jnp.concatenate([vt_ref[head_cols(hh), own], ones_row(T)], axis=0)
        acc_ref[hh] = _dot(vt_own, p_own)

    def weighted(g, carry):
        rows = pl.ds(pl.multiple_of(g * GT, GT), GT)
        for hh in heads:
            p = jnp.exp2(s_ref[hh, rows, :] * log2e_scale - m_scaled[hh]).astype(BF16)
            vt_aug = jnp.concatenate([vt_ref[head_cols(hh), rows], ones_row(GT)], axis=0)
            acc_ref[hh] += _dot(vt_aug, p)
        return carry

    lax.fori_loop(0, n_groups, weighted, 0)
    for hh in heads:
        out_t = acc_ref[hh, :DH, :] / acc_ref[hh, DH:DH + 1, :]
        o_ref[:, head_cols(hh)] = out_t.T.astype(o_ref.dtype)


def _moba(qt, k, vt, kmean, batch, seq, *, name):
    D = k.shape[1]
    n_heads = D // HEAD_DIM
    nb = seq // MOBA_BLOCK
    assert nb % MOBA_GROUP == 0 and nb + 8 <= MOBA_FEATURES and nb % V7X_SUBLANES == 0
    assert n_heads % MOBA_HEADS == 0
    T = MOBA_BLOCK
    W = MOBA_HEADS * HEAD_DIM
    return pl.pallas_call(
        functools.partial(_moba_kernel, n_blocks=nb),
        out_shape=jax.ShapeDtypeStruct((batch * seq, D), BF16),
        grid=(batch, n_heads // MOBA_HEADS, nb),
        in_specs=[pl.BlockSpec((W, T), lambda b, h, i: (h, b * nb + i)),
                  pl.BlockSpec((seq, W), lambda b, h, i: (b, h)),
                  pl.BlockSpec((W, seq), lambda b, h, i: (h, b)),
                  pl.BlockSpec((nb, W), lambda b, h, i: (b, h)),
                  pl.BlockSpec((seq, MOBA_FEATURES), lambda b, h, i: (0, 0)),
                  pl.BlockSpec((MOBA_HEADS, 8, T), lambda b, h, i: (h, 0, 0))],
        out_specs=pl.BlockSpec((T, W), lambda b, h, i: (b * nb + i, h)),
        scratch_shapes=[pltpu.VMEM((MOBA_HEADS, (nb + 1) * T, T), F32),
                        pltpu.VMEM((MOBA_HEADS, 2 * HEAD_DIM, T), F32)],
        compiler_params=_params(("parallel", "parallel", "arbitrary")),
        name=name,
    )(qt, k, vt, kmean, _moba_key_features(seq), _moba_slope_features(n_heads))


def _conv_ffn_block(h_f32, h_bf, w_up, conv_w, conv_b, w_down, g, b, alpha, seq, tag):
    act = _ffn_up(h_bf, w_up.astype(BF16), conv_w, conv_b, seq, name=f"ffn_up_{tag}")
    return _matmul_res_ln(act, w_down.astype(BF16), h_f32, g, b, alpha, tm=256,
                          name=f"ffn_down_ln_{tag}")


def kernel(x, mamba_w_in, mamba_conv_w, mamba_conv_b, mamba_dt_bias, mamba_a_log, mamba_d,
           mamba_norm_w, mamba_w_out, ffn_w_up, ffn_conv_w, ffn_conv_b, ffn_w_down,
           ln_mix_g, ln_mix_b, ln_ffn_g, ln_ffn_b, w_kv, attn_w_q, attn_w_o):
    batch, seq, d_model = x.shape
    depth = ffn_w_up.shape[0]
    n_a = mamba_w_in.shape[0]
    alpha = (2.0 * depth) ** 0.25
    M = batch * seq

    h = x.reshape(M, d_model)
    h_bf = h.astype(BF16)
    k = v = kmean = None
    for layer in range(depth):
        if layer < n_a:
            d_inner = mamba_norm_w.shape[1]
            n_zx = mamba_w_in.shape[2] - mamba_dt_bias.shape[1]
            w_in = mamba_w_in[layer]
            z_act = _matmul(h_bf, w_in[:, :d_inner].astype(BF16), BF16, silu=True,
                            name=f"in_proj_z_{layer}")
            xbc_act = _matmul_conv_silu(h_bf, w_in[:, d_inner:n_zx].astype(BF16),
                                        mamba_conv_w[layer], mamba_conv_b[layer], seq,
                                        name=f"in_proj_xbc_{layer}")
            dt_raw = _matmul(h_bf, w_in[:, n_zx:].astype(BF16), F32, name=f"dt_proj_{layer}")
            y = _ssd(z_act, xbc_act, dt_raw, mamba_dt_bias[layer], mamba_a_log[layer],
                     mamba_d[layer], mamba_norm_w[layer], batch, seq, name=f"ssd_{layer}")
            mix_in, w_mix = y, mamba_w_out[layer]
        else:
            j = layer - n_a
            q = _matmul(h_bf, attn_w_q[j].astype(BF16), BF16, name=f"q_proj_{j}")
            o = _moba(q.T, k, v.T, kmean, batch, seq, name=f"moba_{j}")
            mix_in, w_mix = o, attn_w_o[j]
        h, h_bf = _matmul_res_ln(mix_in, w_mix.astype(BF16), h, ln_mix_g[layer], ln_mix_b[layer],
                                 alpha, tm=512, name=f"mix_out_ln_{layer}")
        h, h_bf = _conv_ffn_block(h, h_bf, ffn_w_up[layer], ffn_conv_w[layer], ffn_conv_b[layer],
                                  ffn_w_down[layer], ln_ffn_g[layer], ln_ffn_b[layer], alpha, seq,
                                  layer)
        if layer == n_a - 1:
            k, kmean = _k_proj(h_bf, w_kv[:, :d_model].astype(BF16), name="k_proj")
            v = _matmul(h_bf, w_kv[:, d_model:].astype(BF16), BF16, name="v_proj")
    return h.reshape(batch, seq, d_model)
```

```python
import functools
import math

import jax
import jax.numpy as jnp
from jax import lax
from jax.experimental import pallas as pl
from jax.experimental.pallas import tpu as pltpu

F32 = jnp.float32
BF16 = jnp.bfloat16

SSM_HEAD_DIM = 64
D_STATE = 128
N_SSM_GROUPS = 8
SSD_CHUNK = 128
HEAD_DIM = 128
MOBA_BLOCK = 256
MOBA_TOPK = 3
LN_EPS = 1e-5
RMS_EPS = 1e-5

V7X_LANES = 128
V7X_SUBLANES = 8
V7X_VMEM_LIMIT_BYTES = 56 * 1024 * 1024


def _tile(dim, target, unit=V7X_LANES):
    if dim <= target:
        return dim
    best = None
    t = unit
    while t <= target:
        if dim % t == 0:
            best = t
        t += unit
    assert best is not None, (dim, target, unit)
    return best


def _params(semantics):
    return pltpu.CompilerParams(dimension_semantics=semantics,
                                vmem_limit_bytes=V7X_VMEM_LIMIT_BYTES)


def _dot(a, b):
    return jnp.dot(a, b, preferred_element_type=F32)


def _dot_nt(a, b):
    return lax.dot_general(a, b, (((1,), (1,)), ((), ())), preferred_element_type=F32)


def _split3(v):
    hi = v.astype(BF16)
    r1 = v - hi.astype(F32)
    mid = r1.astype(BF16)
    lo = (r1 - mid.astype(F32)).astype(BF16)
    return hi, mid, lo


def _silu(v):
    return v * jax.nn.sigmoid(v)


def _cast_weight_once(w_ref, wbf_ref):
    @pl.when(pl.program_id(1) == 0)
    def _():
        wbf_ref[...] = w_ref[...].astype(BF16)


def _proj_kernel(x_ref, w_ref, *refs, silu, transpose_out, block_means):
    if block_means:
        o_ref, km_ref, wbf_ref = refs
    else:
        o_ref, wbf_ref = refs
    _cast_weight_once(w_ref, wbf_ref)
    acc = _dot(x_ref[...], wbf_ref[...])
    if block_means:
        nb, tn = km_ref.shape[1], km_ref.shape[2]
        km_ref[0] = jnp.mean(acc.reshape(nb, MOBA_BLOCK, tn), axis=1)
    if silu:
        acc = _silu(acc)
    out = acc.astype(o_ref.dtype)
    o_ref[...] = out.T if transpose_out else out


def _proj(x, w, layer, col0, n_cols, out_dtype, *, silu=False, transpose_out=False,
          block_means=False, tm=1024, tn=1024, name):
    M, K = x.shape
    tm = _tile(M, tm, MOBA_BLOCK if block_means else V7X_LANES)
    tn = _tile(n_cols, tn)
    assert col0 % tn == 0
    off = col0 // tn
    if transpose_out:
        out_shape = [jax.ShapeDtypeStruct((n_cols, M), out_dtype)]
        out_specs = [pl.BlockSpec((tn, tm), lambda n, m: (n, m))]
    else:
        out_shape = [jax.ShapeDtypeStruct((M, n_cols), out_dtype)]
        out_specs = [pl.BlockSpec((tm, tn), lambda n, m: (m, n))]
    if block_means:
        nb = tm // MOBA_BLOCK
        out_shape.append(jax.ShapeDtypeStruct((M // tm, nb, n_cols), F32))
        out_specs.append(pl.BlockSpec((1, nb, tn), lambda n, m: (m, 0, n)))
    outs = pl.pallas_call(
        functools.partial(_proj_kernel, silu=silu, transpose_out=transpose_out,
                          block_means=block_means),
        out_shape=tuple(out_shape),
        grid=(n_cols // tn, M // tm),
        in_specs=[pl.BlockSpec((tm, K), lambda n, m: (m, 0)),
                  pl.BlockSpec((None, K, tn), lambda n, m: (layer, 0, n + off))],
        out_specs=tuple(out_specs),
        scratch_shapes=[pltpu.VMEM((K, tn), BF16)],
        compiler_params=_params(("parallel", "arbitrary")),
        name=name,
    )(x, w)
    if block_means:
        return outs[0], outs[1].reshape(M // MOBA_BLOCK, n_cols)
    return outs[0]


def _mm_ln_kernel(x_ref, w_ref, h_ref, g_ref, b_ref, of_ref, ob_ref, *, alpha):
    y = alpha * h_ref[...] + _dot(x_ref[...], w_ref[...])
    mu = jnp.mean(y, axis=-1, keepdims=True)
    d = y - mu
    var = jnp.mean(d * d, axis=-1, keepdims=True)
    out = d * lax.rsqrt(var + LN_EPS) * g_ref[...] + b_ref[...]
    of_ref[...] = out
    ob_ref[...] = out.astype(BF16)


def _matmul_res_ln(x, w, h, g, b, alpha, *, tm, name):
    M, K = x.shape
    N = w.shape[1]
    tm = _tile(M, tm)
    return pl.pallas_call(
        functools.partial(_mm_ln_kernel, alpha=alpha),
        out_shape=(jax.ShapeDtypeStruct((M, N), F32), jax.ShapeDtypeStruct((M, N), BF16)),
        grid=(M // tm,),
        in_specs=[pl.BlockSpec((tm, K), lambda m: (m, 0)),
                  pl.BlockSpec((K, N), lambda m: (0, 0), pipeline_mode=pl.Buffered(1)),
                  pl.BlockSpec((tm, N), lambda m: (m, 0)),
                  pl.BlockSpec((1, N), lambda m: (0, 0)),
                  pl.BlockSpec((1, N), lambda m: (0, 0))],
        out_specs=(pl.BlockSpec((tm, N), lambda m: (m, 0)),
                   pl.BlockSpec((tm, N), lambda m: (m, 0))),
        compiler_params=_params(("parallel",)),
        name=name,
    )(x, w, h, g.reshape(1, N), b.reshape(1, N))


CONV_COLS = 256
CONV_ROWS = 512


def _runtime_zero_rows(step):
    start = jnp.minimum(step, 0) * V7X_SUBLANES
    return pl.ds(pl.multiple_of(start, V7X_SUBLANES), CONV_ROWS)


def _pipelined_pieces(tile_shape, produce, consume):
    tm, tn = tile_shape
    pieces = [(slice(r, r + CONV_ROWS), slice(c, c + CONV_COLS))
              for c in range(0, tn, CONV_COLS) for r in range(0, tm, CONV_ROWS)]
    produce(0, *pieces[0])
    for j in range(1, len(pieces)):
        produce(j, *pieces[j])
        consume(j - 1, *pieces[j - 1])
    consume(len(pieces) - 1, *pieces[-1])


def _shift_rows(u, tail, k):
    tm, cw = u.shape
    groups = jnp.concatenate([tail, u], axis=0).reshape(tm // V7X_SUBLANES + 1, V7X_SUBLANES, cw)
    rolled = pltpu.roll(groups, k, axis=1)
    sub = lax.broadcasted_iota(jnp.int32, (1, V7X_SUBLANES, cw), 1)
    out = jnp.where(sub < k, rolled[:-1], rolled[1:])
    return out.reshape(tm, cw)


def _conv_rows(u, tail, w_ref, b_ref, cols):
    K = w_ref.shape[0]
    out = b_ref[:, cols] + w_ref[K - 1:K, cols] * u
    for k in range(1, K):
        out = out + w_ref[K - 1 - k:K - k, cols] * _shift_rows(u, tail, k)
    return out


def _mm_conv_silu_kernel(x_ref, w_ref, cw_ref, cb_ref, o_ref, tail_ref, ua_ref, ub_ref, wbf_ref,
                         *, tiles_per_seq):
    m = pl.program_id(1)
    tn = o_ref.shape[1]

    @pl.when(m % tiles_per_seq == 0)
    def _():
        tail_ref[...] = jnp.zeros_like(tail_ref)

    _cast_weight_once(w_ref, wbf_ref)
    bufs = (ua_ref, ub_ref)
    at = _runtime_zero_rows(m)

    def produce(j, rows, cols):
        bufs[j % 2][at, :] = _dot(x_ref[rows, :], wbf_ref[:, cols])

    def consume(j, rows, cols):
        u = bufs[j % 2][at, :]
        out = _silu(_conv_rows(u, tail_ref[:, cols], cw_ref, cb_ref, cols))
        o_ref[rows, cols] = out.astype(o_ref.dtype)
        tail_ref[:, cols] = u[-V7X_SUBLANES:]

    _pipelined_pieces(o_ref.shape, produce, consume)


def _matmul_conv_silu(x, w, layer, col0, conv_w, conv_b, seq, *, tm=1024, tn=1024, name):
    M, K = x.shape
    N = conv_w.shape[1]
    tm = _tile(seq, tm, V7X_SUBLANES)
    tn = _tile(N, tn)
    assert col0 % tn == 0
    off = col0 // tn
    kc = conv_w.shape[0]
    return pl.pallas_call(
        functools.partial(_mm_conv_silu_kernel, tiles_per_seq=seq // tm),
        out_shape=jax.ShapeDtypeStruct((M, N), BF16),
        grid=(N // tn, M // tm),
        in_specs=[pl.BlockSpec((tm, K), lambda n, m: (m, 0)),
                  pl.BlockSpec((None, K, tn), lambda n, m: (layer, 0, n + off)),
                  pl.BlockSpec((kc, tn), lambda n, m: (0, n)),
                  pl.BlockSpec((1, tn), lambda n, m: (0, n))],
        out_specs=pl.BlockSpec((tm, tn), lambda n, m: (m, n)),
        scratch_shapes=[pltpu.VMEM((V7X_SUBLANES, tn), F32),
                        pltpu.VMEM((CONV_ROWS, CONV_COLS), F32),
                        pltpu.VMEM((CONV_ROWS, CONV_COLS), F32),
                        pltpu.VMEM((K, tn), BF16)],
        compiler_params=_params(("parallel", "arbitrary")),
        name=name,
    )(x, w, conv_w, conv_b.reshape(1, N))


def _ffn_up_kernel(x_ref, wg_ref, wv_ref, cwg_ref, cwv_ref, cbg_ref, cbv_ref, o_ref,
                   tailg_ref, tailv_ref, wgbf_ref, wvbf_ref, *, tiles_per_seq):
    m = pl.program_id(1)

    @pl.when(m % tiles_per_seq == 0)
    def _():
        tailg_ref[...] = jnp.zeros_like(tailg_ref)
        tailv_ref[...] = jnp.zeros_like(tailv_ref)

    _cast_weight_once(wg_ref, wgbf_ref)
    _cast_weight_once(wv_ref, wvbf_ref)
    x = x_ref[...]
    ug = _dot(x, wgbf_ref[...])
    uv = _dot(x, wvbf_ref[...])
    cols = slice(None)
    gate = _conv_rows(ug, tailg_ref[...], cwg_ref, cbg_ref, cols)
    val = _conv_rows(uv, tailv_ref[...], cwv_ref, cbv_ref, cols)
    o_ref[...] = (_silu(gate) * val).astype(o_ref.dtype)
    tailg_ref[...] = ug[-V7X_SUBLANES:]
    tailv_ref[...] = uv[-V7X_SUBLANES:]


def _ffn_up(x, w_up, layer, conv_w, conv_b, seq, *, tm=1024, tn=512, name):
    M, K = x.shape
    d_ff = w_up.shape[2] // 2
    tm = _tile(seq, tm, V7X_SUBLANES)
    tn = _tile(d_ff, tn)
    nn = d_ff // tn
    kc = conv_w.shape[0]
    conv_b = conv_b.reshape(1, 2 * d_ff)
    return pl.pallas_call(
        functools.partial(_ffn_up_kernel, tiles_per_seq=seq // tm),
        out_shape=jax.ShapeDtypeStruct((M, d_ff), BF16),
        grid=(nn, M // tm),
        in_specs=[pl.BlockSpec((tm, K), lambda n, m: (m, 0)),
                  pl.BlockSpec((None, K, tn), lambda n, m: (layer, 0, n)),
                  pl.BlockSpec((None, K, tn), lambda n, m: (layer, 0, n + nn)),
                  pl.BlockSpec((kc, tn), lambda n, m: (0, n)),
                  pl.BlockSpec((kc, tn), lambda n, m: (0, n + nn)),
                  pl.BlockSpec((1, tn), lambda n, m: (0, n)),
                  pl.BlockSpec((1, tn), lambda n, m: (0, n + nn))],
        out_specs=pl.BlockSpec((tm, tn), lambda n, m: (m, n)),
        scratch_shapes=[pltpu.VMEM((V7X_SUBLANES, tn), F32),
                        pltpu.VMEM((V7X_SUBLANES, tn), F32),
                        pltpu.VMEM((K, tn), BF16),
                        pltpu.VMEM((K, tn), BF16)],
        compiler_params=_params(("parallel", "arbitrary")),
        name=name,
    )(x, w_up, w_up, conv_w, conv_w, conv_b, conv_b)


def _ssd_kernel(z_ref, xs_ref, bc_ref, dt_ref, dtb_ref, alog_ref, dskip_ref, nw_ref, expand_ref,
                o_ref, state_ref, acum_t_ref, y_ref, *, heads_per_group):
    c = pl.program_id(1)
    L = SSD_CHUNK
    P = SSM_HEAD_DIM
    N = D_STATE
    G = N_SSM_GROUPS
    R = heads_per_group
    H = G * R
    GP = R * P

    @pl.when(c == 0)
    def _():
        state_ref[...] = jnp.zeros_like(state_ref)

    dt = jax.nn.softplus(dt_ref[...] + dtb_ref[...])
    adt = dt * (-jnp.exp(alog_ref[...]))

    row = lax.broadcasted_iota(jnp.int32, (L, L), 0)
    col = lax.broadcasted_iota(jnp.int32, (L, L), 1)
    causal = row >= col
    tril = jnp.where(causal, 1.0, 0.0).astype(BF16)
    acum = _dot(jnp.concatenate([tril, tril, tril], axis=1),
                jnp.concatenate(_split3(adt), axis=0))

    eye = jnp.where(lax.broadcasted_iota(jnp.int32, (H, H), 0)
                    == lax.broadcasted_iota(jnp.int32, (H, H), 1), 1.0, 0.0).astype(BF16)
    acum_t_ref[...] = _dot_nt(jnp.concatenate([eye, eye, eye], axis=1),
                              jnp.concatenate(_split3(acum), axis=1))

    acum_last = acum[L - 1:L, :]
    decay_to_end = jnp.exp(acum_last - acum)
    decay_from_start = jnp.exp(acum)
    chunk_decay = jnp.broadcast_to(jnp.exp(acum_last), (V7X_SUBLANES, H))

    stacked = jnp.concatenate([dt, decay_to_end, decay_from_start, chunk_decay], axis=0)
    wide = _dot(jnp.concatenate(_split3(stacked), axis=1), expand_ref[...])
    dt_w = wide[0:L]
    dte_w = wide[L:2 * L]
    dfs_w = wide[2 * L:3 * L]
    cdecay_w = wide[3 * L:3 * L + 1]

    xs = xs_ref[...].astype(F32)
    xdt = xs * dt_w
    xdt_bf = xdt.astype(BF16)
    xend_bf = (xdt * dte_w).astype(BF16)
    low_half = lax.broadcasted_iota(jnp.int32, (L, 2 * P), 1) < P

    for g in range(G):
        gs = slice(g * GP, (g + 1) * GP)
        c_g = bc_ref[:, G * N + g * N:G * N + (g + 1) * N]
        b_gt = bc_ref[:, g * N:(g + 1) * N].astype(F32).T.astype(BF16)
        cb = _dot(c_g, b_gt)
        state_g = state_ref[:, gs]
        y_off = _dot(c_g, state_g.astype(BF16)) * dfs_w[:, gs]
        state_ref[:, gs] = state_g * cdecay_w[:, gs] + _dot(b_gt, xend_bf[:, gs])
        for pair in range(R // 2):
            h0 = g * R + 2 * pair
            m_pair = []
            for h in (h0, h0 + 1):
                seg = acum[:, h:h + 1] - acum_t_ref[h:h + 1, :]
                m_pair.append((cb * jnp.exp(jnp.where(causal, seg, -jnp.inf))).astype(BF16))
            x_pair = xdt_bf[:, h0 * P:(h0 + 2) * P]
            zero = jnp.zeros_like(x_pair)
            rhs = jnp.concatenate([jnp.where(low_half, x_pair, zero),
                                   jnp.where(low_half, zero, x_pair)], axis=0)
            y_ref[:, h0 * P:(h0 + 2) * P] = _dot(jnp.concatenate(m_pair, axis=1), rhs)
        y = y_ref[:, gs] + y_off + dskip_ref[:, gs] * xs[:, gs]
        y = y * z_ref[:, gs].astype(F32)
        ms = jnp.mean(y * y, axis=-1, keepdims=True)
        o_ref[:, gs] = (y * lax.rsqrt(ms + RMS_EPS) * nw_ref[:, gs]).astype(o_ref.dtype)


def _ssd(z_act, xbc_act, dt_raw, dt_bias, a_log, d_skip, norm_w, batch, seq, *, name):
    M, d_inner = z_act.shape
    H = dt_bias.shape[0]
    R = H // N_SSM_GROUPS
    gn2 = 2 * N_SSM_GROUPS * D_STATE
    assert d_inner == H * SSM_HEAD_DIM and d_inner % gn2 == 0 and seq % SSD_CHUNK == 0
    assert R % 2 == 0
    L = SSD_CHUNK
    nc = seq // L
    expand = jnp.repeat(jnp.eye(H, dtype=BF16), SSM_HEAD_DIM, axis=1)
    expand3 = jnp.concatenate([expand, expand, expand], axis=0)
    d_wide = jnp.repeat(d_skip.astype(F32), SSM_HEAD_DIM).reshape(1, d_inner)
    row = lambda b, c: b * nc + c
    const = lambda b, c: (0, 0)
    return pl.pallas_call(
        functools.partial(_ssd_kernel, heads_per_group=R),
        out_shape=jax.ShapeDtypeStruct((M, d_inner), BF16),
        grid=(batch, nc),
        in_specs=[pl.BlockSpec((L, d_inner), lambda b, c: (row(b, c), 0)),
                  pl.BlockSpec((L, d_inner), lambda b, c: (row(b, c), 0)),
                  pl.BlockSpec((L, gn2), lambda b, c: (row(b, c), d_inner // gn2)),
                  pl.BlockSpec((L, H), lambda b, c: (row(b, c), 0)),
                  pl.BlockSpec((1, H), const),
                  pl.BlockSpec((1, H), const),
                  pl.BlockSpec((1, d_inner), const),
                  pl.BlockSpec((1, d_inner), const),
                  pl.BlockSpec((3 * H, d_inner), const)],
        out_specs=pl.BlockSpec((L, d_inner), lambda b, c: (row(b, c), 0)),
        scratch_shapes=[pltpu.VMEM((D_STATE, d_inner), F32),
                        pltpu.VMEM((H, L), F32),
                        pltpu.VMEM((L, d_inner), F32)],
        compiler_params=_params(("parallel", "arbitrary")),
        name=name,
    )(z_act, xbc_act, xbc_act, dt_raw,
      dt_bias.reshape(1, H).astype(F32), a_log.reshape(1, H).astype(F32), d_wide,
      norm_w.reshape(1, d_inner).astype(F32), expand3)


MOBA_MASK_VALUE = -1e30
MOBA_FEATURES = 128
MOBA_GROUP = 4
MOBA_HEADS = 2


def _moba_key_features(seq):
    nb = seq // MOBA_BLOCK
    pos = jnp.arange(seq)
    blk = pos // MOBA_BLOCK
    onehot = (blk[:, None] == jnp.arange(nb)[None, :]).astype(F32)
    coarse = jnp.broadcast_to((blk * MOBA_BLOCK).astype(F32)[:, None], (seq, 3))
    fine = jnp.broadcast_to((pos % MOBA_BLOCK).astype(F32)[:, None], (seq, 3))
    zeros = jnp.zeros((seq, MOBA_FEATURES - nb - 6), F32)
    return jnp.concatenate([onehot, coarse, fine, zeros], axis=1).astype(BF16)


def _moba_slope_features(n_heads):
    slope = 2.0 ** (-8.0 * jnp.arange(1, n_heads + 1, dtype=F32) / n_heads) / (HEAD_DIM ** -0.5)
    hi = slope.astype(BF16).astype(F32)
    mid = (slope - hi).astype(BF16).astype(F32)
    lo = (slope - hi - mid).astype(BF16).astype(F32)
    rows = jnp.stack([hi, mid, lo, hi, mid, lo, slope, jnp.zeros_like(slope)], axis=1)
    return jnp.broadcast_to(rows[:, :, None], (n_heads, 8, MOBA_BLOCK))


def _fold_max(s):
    return jnp.max(s.reshape(s.shape[0] // V7X_SUBLANES, V7X_SUBLANES, s.shape[1]), axis=0)


def _moba_kernel(qt_ref, k_ref, vt_ref, km_ref, kfeat_ref, sfeat_ref, o_ref,
                 s_ref, acc_ref, *, n_blocks):
    i = pl.program_id(2)
    T = MOBA_BLOCK
    GT = MOBA_GROUP * T
    DH = HEAD_DIM
    NB = n_blocks
    heads = range(MOBA_HEADS)
    log2e_scale = (HEAD_DIM ** -0.5) * math.log2(math.e)
    own = pl.ds(pl.multiple_of(i * T, T), T)
    own_slot = pl.ds(NB * T, T)
    n_groups = (i + MOBA_GROUP - 1) // MOBA_GROUP

    def head_cols(hh):
        return slice(hh * DH, (hh + 1) * DH)

    def ones_row(width):
        first = lax.broadcasted_iota(jnp.int32, (DH, width), 0) == 0
        return jnp.where(first, 1.0, 0.0).astype(BF16)

    def select(hh):
        qt = qt_ref[head_cols(hh), :]
        gate = _dot(km_ref[:, head_cols(hh)].astype(BF16), qt)
        blk = lax.broadcasted_iota(jnp.int32, gate.shape, 0).astype(F32)
        valid = blk < i.astype(F32)
        neg_inf = jnp.float32(-jnp.inf)
        chosen = jnp.zeros(gate.shape, dtype=jnp.bool_)
        for _ in range(min(MOBA_TOPK, NB)):
            cur = jnp.where(jnp.logical_and(valid, jnp.logical_not(chosen)), gate, neg_inf)
            best = jnp.max(cur, axis=0, keepdims=True)
            hit = jnp.logical_and(cur == best, cur > neg_inf)
            first = jnp.min(jnp.where(hit, blk, float(NB)), axis=0, keepdims=True)
            chosen = jnp.logical_or(chosen, blk == first)
        mask_rows = jnp.where(chosen, 0.0, MOBA_MASK_VALUE)
        feat = jnp.concatenate(
            [mask_rows, sfeat_ref[hh], jnp.zeros((MOBA_FEATURES - NB - 8, T), F32)], axis=0)
        return jnp.concatenate([qt, feat.astype(BF16)], axis=0)

    q_aug = [select(hh) for hh in heads]

    key_idx = lax.broadcasted_iota(jnp.int32, (T, T), 0)
    qry_idx = lax.broadcasted_iota(jnp.int32, (T, T), 1)
    key_pos = (key_idx + i * T).astype(F32)
    mx0 = []
    for hh in heads:
        s_own = _dot(k_ref[own, head_cols(hh)], qt_ref[head_cols(hh), :])
        s_own = s_own + sfeat_ref[hh, 6:7, :] * key_pos
        s_own = jnp.where(key_idx <= qry_idx, s_own, MOBA_MASK_VALUE)
        s_ref[hh, own_slot, :] = s_own
        mx0.append(_fold_max(s_own))

    def scores(g, mx):
        rows = pl.ds(pl.multiple_of(g * GT, GT), GT)
        kfeat = kfeat_ref[rows, :]
        new = []
        for hh in heads:
            k_aug = jnp.concatenate([k_ref[rows, head_cols(hh)], kfeat], axis=1)
            s = _dot(k_aug, q_aug[hh])
            s_ref[hh, rows, :] = s
            new.append(jnp.maximum(mx[hh], _fold_max(s)))
        return tuple(new)

    mx = lax.fori_loop(0, n_groups, scores, tuple(mx0))
    m_scaled = [jnp.max(mx[hh], axis=0, keepdims=True) * log2e_scale for hh in heads]

    for hh in heads:
        p_own = jnp.exp2(s_ref[hh, own_slot, :] * log2e_scale - m_scaled[hh]).astype(BF16)
        vt_own = jnp.concatenate([vt_ref[head_cols(hh), own], ones_row(T)], axis=0)
        acc_ref[hh] = _dot(vt_own, p_own)

    def weighted(g, carry):
        rows = pl.ds(pl.multiple_of(g * GT, GT), GT)
        for hh in heads:
            p = jnp.exp2(s_ref[hh, rows, :] * log2e_scale - m_scaled[hh]).astype(BF16)
            vt_aug = jnp.concatenate([vt_ref[head_cols(hh), rows], ones_row(GT)], axis=0)
            acc_ref[hh] += _dot(vt_aug, p)
        return carry

    lax.fori_loop(0, n_groups, weighted, 0)
    for hh in heads:
        out_t = acc_ref[hh, :DH, :] / acc_ref[hh, DH:DH + 1, :]
        o_ref[:, head_cols(hh)] = out_t.T.astype(o_ref.dtype)


def _moba(qt, k, vt, kmean, batch, seq, *, name):
    D = k.shape[1]
    n_heads = D // HEAD_DIM
    nb = seq // MOBA_BLOCK
    assert nb % MOBA_GROUP == 0 and nb + 8 <= MOBA_FEATURES and nb % V7X_SUBLANES == 0
    assert n_heads % MOBA_HEADS == 0
    T = MOBA_BLOCK
    W = MOBA_HEADS * HEAD_DIM
    return pl.pallas_call(
        functools.partial(_moba_kernel, n_blocks=nb),
        out_shape=jax.ShapeDtypeStruct((batch * seq, D), BF16),
        grid=(batch, n_heads // MOBA_HEADS, nb),
        in_specs=[pl.BlockSpec((W, T), lambda b, h, i: (h, b * nb + i)),
                  pl.BlockSpec((seq, W), lambda b, h, i: (b, h)),
                  pl.BlockSpec((W, seq), lambda b, h, i: (h, b)),
                  pl.BlockSpec((nb, W), lambda b, h, i: (b, h)),
                  pl.BlockSpec((seq, MOBA_FEATURES), lambda b, h, i: (0, 0)),
                  pl.BlockSpec((MOBA_HEADS, 8, T), lambda b, h, i: (h, 0, 0))],
        out_specs=pl.BlockSpec((T, W), lambda b, h, i: (b * nb + i, h)),
        scratch_shapes=[pltpu.VMEM((MOBA_HEADS, (nb + 1) * T, T), F32),
                        pltpu.VMEM((MOBA_HEADS, 2 * HEAD_DIM, T), F32)],
        compiler_params=_params(("parallel", "parallel", "arbitrary")),
        name=name,
    )(qt, k, vt, kmean, _moba_key_features(seq), _moba_slope_features(n_heads))


def _conv_ffn_block(h_f32, h_bf, w_up, layer, conv_w, conv_b, w_down, g, b, alpha, seq):
    act = _ffn_up(h_bf, w_up, layer, conv_w, conv_b, seq, name=f"ffn_up_{layer}")
    return _matmul_res_ln(act, w_down.astype(BF16), h_f32, g, b, alpha, tm=256,
                          name=f"ffn_down_ln_{layer}")


def kernel(x, mamba_w_in, mamba_conv_w, mamba_conv_b, mamba_dt_bias, mamba_a_log, mamba_d,
           mamba_norm_w, mamba_w_out, ffn_w_up, ffn_conv_w, ffn_conv_b, ffn_w_down,
           ln_mix_g, ln_mix_b, ln_ffn_g, ln_ffn_b, w_kv, attn_w_q, attn_w_o):
    batch, seq, d_model = x.shape
    depth = ffn_w_up.shape[0]
    n_a = mamba_w_in.shape[0]
    alpha = (2.0 * depth) ** 0.25
    M = batch * seq

    h = x.reshape(M, d_model)
    h_bf = h.astype(BF16)
    k = vt = kmean = None
    for layer in range(depth):
        if layer < n_a:
            d_inner = mamba_norm_w.shape[1]
            n_zx = mamba_w_in.shape[2] - mamba_dt_bias.shape[1]
            z_act = _proj(h_bf, mamba_w_in, layer, 0, d_inner, BF16, silu=True,
                          name=f"in_proj_z_{layer}")
            xbc_act = _matmul_conv_silu(h_bf, mamba_w_in, layer, d_inner,
                                        mamba_conv_w[layer], mamba_conv_b[layer], seq,
                                        name=f"in_proj_xbc_{layer}")
            w_dt = mamba_w_in[layer:layer + 1, :, n_zx:]
            dt_raw = _proj(h_bf, w_dt, 0, 0, w_dt.shape[2], F32, name=f"dt_proj_{layer}")
            y = _ssd(z_act, xbc_act, dt_raw, mamba_dt_bias[layer], mamba_a_log[layer],
                     mamba_d[layer], mamba_norm_w[layer], batch, seq, name=f"ssd_{layer}")
            mix_in, w_mix = y, mamba_w_out[layer]
        else:
            j = layer - n_a
            qt = _proj(h_bf, attn_w_q, j, 0, d_model, BF16, transpose_out=True,
                       name=f"q_proj_{j}")
            o = _moba(qt, k, vt, kmean, batch, seq, name=f"moba_{j}")
            mix_in, w_mix = o, attn_w_o[j]
        h, h_bf = _matmul_res_ln(mix_in, w_mix.astype(BF16), h, ln_mix_g[layer], ln_mix_b[layer],
                                 alpha, tm=512, name=f"mix_out_ln_{layer}")
        h, h_bf = _conv_ffn_block(h, h_bf, ffn_w_up, layer, ffn_conv_w[layer], ffn_conv_b[layer],
                                  ffn_w_down[layer], ln_ffn_g[layer], ln_ffn_b[layer], alpha, seq)
        if layer == n_a - 1:
            w_kv3 = w_kv.reshape(1, d_model, 2 * d_model)
            k, kmean = _proj(h_bf, w_kv3, 0, 0, d_model, BF16, block_means=True, name="k_proj")
            vt = _proj(h_bf, w_kv3, 0, d_model, d_model, BF16, transpose_out=True, name="v_proj")
    return h.reshape(batch, seq, d_model)
```

```python
import functools
import math

import jax
import jax.numpy as jnp
from jax import lax
from jax.experimental import pallas as pl
from jax.experimental.pallas import tpu as pltpu

F32 = jnp.float32
BF16 = jnp.bfloat16

SSM_HEAD_DIM = 64
D_STATE = 128
N_SSM_GROUPS = 8
SSD_CHUNK = 128
HEAD_DIM = 128
MOBA_BLOCK = 256
MOBA_TOPK = 3
LN_EPS = 1e-5
RMS_EPS = 1e-5

V7X_LANES = 128
V7X_SUBLANES = 8
V7X_VMEM_LIMIT_BYTES = 56 * 1024 * 1024


def _tile(dim, target, unit=V7X_LANES):
    if dim <= target:
        return dim
    best = None
    t = unit
    while t <= target:
        if dim % t == 0:
            best = t
        t += unit
    assert best is not None, (dim, target, unit)
    return best


def _params(semantics):
    return pltpu.CompilerParams(dimension_semantics=semantics,
                                vmem_limit_bytes=V7X_VMEM_LIMIT_BYTES)


def _dot(a, b):
    return jnp.dot(a, b, preferred_element_type=F32)


def _dot_nt(a, b):
    return lax.dot_general(a, b, (((1,), (1,)), ((), ())), preferred_element_type=F32)


def _split3(v):
    hi = v.astype(BF16)
    r1 = v - hi.astype(F32)
    mid = r1.astype(BF16)
    lo = (r1 - mid.astype(F32)).astype(BF16)
    return hi, mid, lo


def _silu(v):
    return v * jax.nn.sigmoid(v)


def _cast_weight_once(w_ref, wbf_ref):
    @pl.when(pl.program_id(1) == 0)
    def _():
        wbf_ref[...] = w_ref[...].astype(BF16)


def _proj_kernel(x_ref, w_ref, *refs, silu, transpose_out, block_means):
    if block_means:
        o_ref, km_ref, wbf_ref = refs
    else:
        o_ref, wbf_ref = refs
    _cast_weight_once(w_ref, wbf_ref)
    acc = _dot(x_ref[...], wbf_ref[...])
    if block_means:
        nb, tn = km_ref.shape[1], km_ref.shape[2]
        km_ref[0] = jnp.mean(acc.reshape(nb, MOBA_BLOCK, tn), axis=1)
    if silu:
        acc = _silu(acc)
    out = acc.astype(o_ref.dtype)
    o_ref[...] = out.T if transpose_out else out


def _proj(x, w, layer, col0, n_cols, out_dtype, *, silu=False, transpose_out=False,
          block_means=False, tm=1024, tn=1024, name):
    M, K = x.shape
    tm = _tile(M, tm, MOBA_BLOCK if block_means else V7X_LANES)
    tn = _tile(n_cols, tn)
    assert col0 % tn == 0
    off = col0 // tn
    if transpose_out:
        out_shape = [jax.ShapeDtypeStruct((n_cols, M), out_dtype)]
        out_specs = [pl.BlockSpec((tn, tm), lambda n, m: (n, m))]
    else:
        out_shape = [jax.ShapeDtypeStruct((M, n_cols), out_dtype)]
        out_specs = [pl.BlockSpec((tm, tn), lambda n, m: (m, n))]
    if block_means:
        nb = tm // MOBA_BLOCK
        out_shape.append(jax.ShapeDtypeStruct((M // tm, nb, n_cols), F32))
        out_specs.append(pl.BlockSpec((1, nb, tn), lambda n, m: (m, 0, n)))
    outs = pl.pallas_call(
        functools.partial(_proj_kernel, silu=silu, transpose_out=transpose_out,
                          block_means=block_means),
        out_shape=tuple(out_shape),
        grid=(n_cols // tn, M // tm),
        in_specs=[pl.BlockSpec((tm, K), lambda n, m: (m, 0)),
                  pl.BlockSpec((None, K, tn), lambda n, m: (layer, 0, n + off))],
        out_specs=tuple(out_specs),
        scratch_shapes=[pltpu.VMEM((K, tn), BF16)],
        compiler_params=_params(("parallel", "arbitrary")),
        name=name,
    )(x, w)
    if block_means:
        return outs[0], outs[1].reshape(M // MOBA_BLOCK, n_cols)
    return outs[0]


def _mm_ln_kernel(x_ref, w_ref, h_ref, g_ref, b_ref, of_ref, ob_ref, *, alpha):
    y = alpha * h_ref[...] + _dot(x_ref[...], w_ref[...])
    mu = jnp.mean(y, axis=-1, keepdims=True)
    d = y - mu
    var = jnp.mean(d * d, axis=-1, keepdims=True)
    out = d * lax.rsqrt(var + LN_EPS) * g_ref[...] + b_ref[...]
    of_ref[...] = out
    ob_ref[...] = out.astype(BF16)


def _matmul_res_ln(x, w, layer, h, g, b, alpha, *, tm, name):
    M, K = x.shape
    N = w.shape[2]
    tm = _tile(M, tm)
    return pl.pallas_call(
        functools.partial(_mm_ln_kernel, alpha=alpha),
        out_shape=(jax.ShapeDtypeStruct((M, N), F32), jax.ShapeDtypeStruct((M, N), BF16)),
        grid=(M // tm,),
        in_specs=[pl.BlockSpec((tm, K), lambda m: (m, 0)),
                  pl.BlockSpec((None, K, N), lambda m: (layer, 0, 0),
                               pipeline_mode=pl.Buffered(1)),
                  pl.BlockSpec((tm, N), lambda m: (m, 0)),
                  pl.BlockSpec((1, N), lambda m: (0, 0)),
                  pl.BlockSpec((1, N), lambda m: (0, 0))],
        out_specs=(pl.BlockSpec((tm, N), lambda m: (m, 0)),
                   pl.BlockSpec((tm, N), lambda m: (m, 0))),
        compiler_params=_params(("parallel",)),
        name=name,
    )(x, w, h, g.reshape(1, N), b.reshape(1, N))


CONV_COLS = 256
CONV_ROWS = 512


def _runtime_zero_rows(step):
    start = jnp.minimum(step, 0) * V7X_SUBLANES
    return pl.ds(pl.multiple_of(start, V7X_SUBLANES), CONV_ROWS)


def _pipelined_pieces(tile_shape, produce, consume):
    tm, tn = tile_shape
    pieces = [(slice(r, r + CONV_ROWS), slice(c, c + CONV_COLS))
              for c in range(0, tn, CONV_COLS) for r in range(0, tm, CONV_ROWS)]
    produce(0, *pieces[0])
    for j in range(1, len(pieces)):
        produce(j, *pieces[j])
        consume(j - 1, *pieces[j - 1])
    consume(len(pieces) - 1, *pieces[-1])


def _shift_rows(u, tail, k):
    tm, cw = u.shape
    groups = jnp.concatenate([tail, u], axis=0).reshape(tm // V7X_SUBLANES + 1, V7X_SUBLANES, cw)
    rolled = pltpu.roll(groups, k, axis=1)
    sub = lax.broadcasted_iota(jnp.int32, (1, V7X_SUBLANES, cw), 1)
    out = jnp.where(sub < k, rolled[:-1], rolled[1:])
    return out.reshape(tm, cw)


def _conv_rows(u, tail, w_ref, b_ref, cols):
    K = w_ref.shape[0]
    out = b_ref[:, cols] + w_ref[K - 1:K, cols] * u
    for k in range(1, K):
        out = out + w_ref[K - 1 - k:K - k, cols] * _shift_rows(u, tail, k)
    return out


def _mm_conv_silu_kernel(x_ref, w_ref, cw_ref, cb_ref, o_ref, tail_ref, ua_ref, ub_ref, wbf_ref,
                         *, tiles_per_seq):
    m = pl.program_id(1)
    tn = o_ref.shape[1]

    @pl.when(m % tiles_per_seq == 0)
    def _():
        tail_ref[...] = jnp.zeros_like(tail_ref)

    _cast_weight_once(w_ref, wbf_ref)
    bufs = (ua_ref, ub_ref)
    at = _runtime_zero_rows(m)

    def produce(j, rows, cols):
        bufs[j % 2][at, :] = _dot(x_ref[rows, :], wbf_ref[:, cols])

    def consume(j, rows, cols):
        u = bufs[j % 2][at, :]
        out = _silu(_conv_rows(u, tail_ref[:, cols], cw_ref, cb_ref, cols))
        o_ref[rows, cols] = out.astype(o_ref.dtype)
        tail_ref[:, cols] = u[-V7X_SUBLANES:]

    _pipelined_pieces(o_ref.shape, produce, consume)


def _matmul_conv_silu(x, w, layer, col0, conv_w, conv_b, seq, *, tm=1024, tn=1024, name):
    M, K = x.shape
    N = conv_w.shape[1]
    tm = _tile(seq, tm, V7X_SUBLANES)
    tn = _tile(N, tn)
    assert col0 % tn == 0
    off = col0 // tn
    kc = conv_w.shape[0]
    return pl.pallas_call(
        functools.partial(_mm_conv_silu_kernel, tiles_per_seq=seq // tm),
        out_shape=jax.ShapeDtypeStruct((M, N), BF16),
        grid=(N // tn, M // tm),
        in_specs=[pl.BlockSpec((tm, K), lambda n, m: (m, 0)),
                  pl.BlockSpec((None, K, tn), lambda n, m: (layer, 0, n + off)),
                  pl.BlockSpec((kc, tn), lambda n, m: (0, n)),
                  pl.BlockSpec((1, tn), lambda n, m: (0, n))],
        out_specs=pl.BlockSpec((tm, tn), lambda n, m: (m, n)),
        scratch_shapes=[pltpu.VMEM((V7X_SUBLANES, tn), F32),
                        pltpu.VMEM((CONV_ROWS, CONV_COLS), F32),
                        pltpu.VMEM((CONV_ROWS, CONV_COLS), F32),
                        pltpu.VMEM((K, tn), BF16)],
        compiler_params=_params(("parallel", "arbitrary")),
        name=name,
    )(x, w, conv_w, conv_b.reshape(1, N))


def _ffn_up_kernel(x_ref, wg_ref, wv_ref, cwg_ref, cwv_ref, cbg_ref, cbv_ref, o_ref,
                   tailg_ref, tailv_ref, wgbf_ref, wvbf_ref, *, tiles_per_seq):
    m = pl.program_id(1)

    @pl.when(m % tiles_per_seq == 0)
    def _():
        tailg_ref[...] = jnp.zeros_like(tailg_ref)
        tailv_ref[...] = jnp.zeros_like(tailv_ref)

    _cast_weight_once(wg_ref, wgbf_ref)
    _cast_weight_once(wv_ref, wvbf_ref)
    x = x_ref[...]
    ug = _dot(x, wgbf_ref[...])
    uv = _dot(x, wvbf_ref[...])
    cols = slice(None)
    gate = _conv_rows(ug, tailg_ref[...], cwg_ref, cbg_ref, cols)
    val = _conv_rows(uv, tailv_ref[...], cwv_ref, cbv_ref, cols)
    o_ref[...] = (_silu(gate) * val).astype(o_ref.dtype)
    tailg_ref[...] = ug[-V7X_SUBLANES:]
    tailv_ref[...] = uv[-V7X_SUBLANES:]


def _ffn_up(x, w_up, layer, conv_w, conv_b, seq, *, tm=1024, tn=512, name):
    M, K = x.shape
    d_ff = w_up.shape[2] // 2
    tm = _tile(seq, tm, V7X_SUBLANES)
    tn = _tile(d_ff, tn)
    nn = d_ff // tn
    kc = conv_w.shape[0]
    conv_b = conv_b.reshape(1, 2 * d_ff)
    return pl.pallas_call(
        functools.partial(_ffn_up_kernel, tiles_per_seq=seq // tm),
        out_shape=jax.ShapeDtypeStruct((M, d_ff), BF16),
        grid=(nn, M // tm),
        in_specs=[pl.BlockSpec((tm, K), lambda n, m: (m, 0)),
                  pl.BlockSpec((None, K, tn), lambda n, m: (layer, 0, n)),
                  pl.BlockSpec((None, K, tn), lambda n, m: (layer, 0, n + nn)),
                  pl.BlockSpec((kc, tn), lambda n, m: (0, n)),
                  pl.BlockSpec((kc, tn), lambda n, m: (0, n + nn)),
                  pl.BlockSpec((1, tn), lambda n, m: (0, n)),
                  pl.BlockSpec((1, tn), lambda n, m: (0, n + nn))],
        out_specs=pl.BlockSpec((tm, tn), lambda n, m: (m, n)),
        scratch_shapes=[pltpu.VMEM((V7X_SUBLANES, tn), F32),
                        pltpu.VMEM((V7X_SUBLANES, tn), F32),
                        pltpu.VMEM((K, tn), BF16),
                        pltpu.VMEM((K, tn), BF16)],
        compiler_params=_params(("parallel", "arbitrary")),
        name=name,
    )(x, w_up, w_up, conv_w, conv_w, conv_b, conv_b)


def _ssd_kernel(z_ref, xs_ref, bc_ref, dt_ref, dtb_ref, alog_ref, dskip_ref, nw_ref, expand_ref,
                o_ref, state_ref, acum_t_ref, y_ref, *, heads_per_group):
    c = pl.program_id(1)
    L = SSD_CHUNK
    P = SSM_HEAD_DIM
    N = D_STATE
    G = N_SSM_GROUPS
    R = heads_per_group
    H = G * R
    GP = R * P

    @pl.when(c == 0)
    def _():
        state_ref[...] = jnp.zeros_like(state_ref)

    dt = jax.nn.softplus(dt_ref[...] + dtb_ref[...])
    adt = dt * (-jnp.exp(alog_ref[...]))

    row = lax.broadcasted_iota(jnp.int32, (L, L), 0)
    col = lax.broadcasted_iota(jnp.int32, (L, L), 1)
    causal = row >= col
    tril = jnp.where(causal, 1.0, 0.0).astype(BF16)
    acum = _dot(jnp.concatenate([tril, tril, tril], axis=1),
                jnp.concatenate(_split3(adt), axis=0))

    eye = jnp.where(lax.broadcasted_iota(jnp.int32, (H, H), 0)
                    == lax.broadcasted_iota(jnp.int32, (H, H), 1), 1.0, 0.0).astype(BF16)
    acum_t_ref[...] = _dot_nt(jnp.concatenate([eye, eye, eye], axis=1),
                              jnp.concatenate(_split3(acum), axis=1))

    acum_last = acum[L - 1:L, :]
    decay_to_end = jnp.exp(acum_last - acum)
    decay_from_start = jnp.exp(acum)
    chunk_decay = jnp.broadcast_to(jnp.exp(acum_last), (V7X_SUBLANES, H))

    stacked = jnp.concatenate([dt, decay_to_end, decay_from_start, chunk_decay], axis=0)
    wide = _dot(jnp.concatenate(_split3(stacked), axis=1), expand_ref[...])
    dt_w = wide[0:L]
    dte_w = wide[L:2 * L]
    dfs_w = wide[2 * L:3 * L]
    cdecay_w = wide[3 * L:3 * L + 1]

    xs = xs_ref[...].astype(F32)
    xdt = xs * dt_w
    xdt_bf = xdt.astype(BF16)
    xend_bf = (xdt * dte_w).astype(BF16)
    low_half = lax.broadcasted_iota(jnp.int32, (L, 2 * P), 1) < P

    for g in range(G):
        gs = slice(g * GP, (g + 1) * GP)
        c_g = bc_ref[:, G * N + g * N:G * N + (g + 1) * N]
        b_gt = bc_ref[:, g * N:(g + 1) * N].astype(F32).T.astype(BF16)
        cb = _dot(c_g, b_gt)
        state_g = state_ref[:, gs]
        y_off = _dot(c_g, state_g.astype(BF16)) * dfs_w[:, gs]
        state_ref[:, gs] = state_g * cdecay_w[:, gs] + _dot(b_gt, xend_bf[:, gs])
        for pair in range(R // 2):
            h0 = g * R + 2 * pair
            m_pair = []
            for h in (h0, h0 + 1):
                seg = acum[:, h:h + 1] - acum_t_ref[h:h + 1, :]
                m_pair.append((cb * jnp.exp(jnp.where(causal, seg, -jnp.inf))).astype(BF16))
            x_pair = xdt_bf[:, h0 * P:(h0 + 2) * P]
            zero = jnp.zeros_like(x_pair)
            rhs = jnp.concatenate([jnp.where(low_half, x_pair, zero),
                                   jnp.where(low_half, zero, x_pair)], axis=0)
            y_ref[:, h0 * P:(h0 + 2) * P] = _dot(jnp.concatenate(m_pair, axis=1), rhs)
        y = y_ref[:, gs] + y_off + dskip_ref[:, gs] * xs[:, gs]
        y = y * z_ref[:, gs].astype(F32)
        ms = jnp.mean(y * y, axis=-1, keepdims=True)
        o_ref[:, gs] = (y * lax.rsqrt(ms + RMS_EPS) * nw_ref[:, gs]).astype(o_ref.dtype)


def _ssd(z_act, xbc_act, dt_raw, dt_bias, a_log, d_skip, norm_w, batch, seq, *, name):
    M, d_inner = z_act.shape
    H = dt_bias.shape[0]
    R = H // N_SSM_GROUPS
    gn2 = 2 * N_SSM_GROUPS * D_STATE
    assert d_inner == H * SSM_HEAD_DIM and d_inner % gn2 == 0 and seq % SSD_CHUNK == 0
    assert R % 2 == 0
    L = SSD_CHUNK
    nc = seq // L
    expand = jnp.repeat(jnp.eye(H, dtype=BF16), SSM_HEAD_DIM, axis=1)
    expand3 = jnp.concatenate([expand, expand, expand], axis=0)
    d_wide = jnp.repeat(d_skip.astype(F32), SSM_HEAD_DIM).reshape(1, d_inner)
    row = lambda b, c: b * nc + c
    const = lambda b, c: (0, 0)
    return pl.pallas_call(
        functools.partial(_ssd_kernel, heads_per_group=R),
        out_shape=jax.ShapeDtypeStruct((M, d_inner), BF16),
        grid=(batch, nc),
        in_specs=[pl.BlockSpec((L, d_inner), lambda b, c: (row(b, c), 0)),
                  pl.BlockSpec((L, d_inner), lambda b, c: (row(b, c), 0)),
                  pl.BlockSpec((L, gn2), lambda b, c: (row(b, c), d_inner // gn2)),
                  pl.BlockSpec((L, H), lambda b, c: (row(b, c), 0)),
                  pl.BlockSpec((1, H), const),
                  pl.BlockSpec((1, H), const),
                  pl.BlockSpec((1, d_inner), const),
                  pl.BlockSpec((1, d_inner), const),
                  pl.BlockSpec((3 * H, d_inner), const)],
        out_specs=pl.BlockSpec((L, d_inner), lambda b, c: (row(b, c), 0)),
        scratch_shapes=[pltpu.VMEM((D_STATE, d_inner), F32),
                        pltpu.VMEM((H, L), F32),
                        pltpu.VMEM((L, d_inner), F32)],
        compiler_params=_params(("parallel", "arbitrary")),
        name=name,
    )(z_act, xbc_act, xbc_act, dt_raw,
      dt_bias.reshape(1, H).astype(F32), a_log.reshape(1, H).astype(F32), d_wide,
      norm_w.reshape(1, d_inner).astype(F32), expand3)


MOBA_MASK_VALUE = -1e30
MOBA_FEATURES = 128
MOBA_GROUP = 4
MOBA_HEADS = 4
MOBA_ONES = 16


def _moba_key_features(seq):
    nb = seq // MOBA_BLOCK
    pos = jnp.arange(seq)
    blk = pos // MOBA_BLOCK
    onehot = (blk[:, None] == jnp.arange(nb)[None, :]).astype(F32)
    coarse = jnp.broadcast_to((blk * MOBA_BLOCK).astype(F32)[:, None], (seq, 3))
    fine = jnp.broadcast_to((pos % MOBA_BLOCK).astype(F32)[:, None], (seq, 3))
    zeros = jnp.zeros((seq, MOBA_FEATURES - nb - 6), F32)
    return jnp.concatenate([onehot, coarse, fine, zeros], axis=1).astype(BF16)


def _moba_slope_features(n_heads):
    slope = 2.0 ** (-8.0 * jnp.arange(1, n_heads + 1, dtype=F32) / n_heads) / (HEAD_DIM ** -0.5)
    hi = slope.astype(BF16).astype(F32)
    mid = (slope - hi).astype(BF16).astype(F32)
    lo = (slope - hi - mid).astype(BF16).astype(F32)
    rows = jnp.stack([hi, mid, lo, hi, mid, lo, slope, jnp.zeros_like(slope)], axis=1)
    return jnp.broadcast_to(rows[:, :, None], (n_heads, 8, MOBA_BLOCK))


def _fold_max(s):
    return jnp.max(s.reshape(s.shape[0] // V7X_SUBLANES, V7X_SUBLANES, s.shape[1]), axis=0)


def _fold_sum(s):
    return jnp.sum(s.reshape(s.shape[0] // V7X_SUBLANES, V7X_SUBLANES, s.shape[1]), axis=0)


def _moba_kernel(qt_ref, k_ref, vt_ref, km_ref, kfeat_ref, sfeat_ref, o_ref,
                 s_ref, acc_ref, *, n_blocks):
    i = pl.program_id(2)
    T = MOBA_BLOCK
    GT = MOBA_GROUP * T
    DH = HEAD_DIM
    NB = n_blocks
    heads = range(MOBA_HEADS)
    log2e_scale = (HEAD_DIM ** -0.5) * math.log2(math.e)
    own = pl.ds(pl.multiple_of(i * T, T), T)
    own_slot = pl.ds(NB * T, T)
    n_groups = (i + MOBA_GROUP - 1) // MOBA_GROUP

    def head_cols(hh):
        return slice(hh * DH, (hh + 1) * DH)

    def select(hh):
        qt = qt_ref[head_cols(hh), :]
        gate = _dot(km_ref[:, head_cols(hh)].astype(BF16), qt)
        blk = lax.broadcasted_iota(jnp.int32, gate.shape, 0).astype(F32)
        valid = blk < i.astype(F32)
        neg_inf = jnp.float32(-jnp.inf)
        chosen = jnp.zeros(gate.shape, dtype=jnp.bool_)
        for _ in range(min(MOBA_TOPK, NB)):
            cur = jnp.where(jnp.logical_and(valid, jnp.logical_not(chosen)), gate, neg_inf)
            best = jnp.max(cur, axis=0, keepdims=True)
            hit = jnp.logical_and(cur == best, cur > neg_inf)
            first = jnp.min(jnp.where(hit, blk, float(NB)), axis=0, keepdims=True)
            chosen = jnp.logical_or(chosen, blk == first)
        mask_rows = jnp.where(chosen, 0.0, MOBA_MASK_VALUE)
        feat = jnp.concatenate(
            [mask_rows, sfeat_ref[hh], jnp.zeros((MOBA_FEATURES - NB - 8, T), F32)], axis=0)
        return jnp.concatenate([qt, feat.astype(BF16)], axis=0)

    q_aug = [select(hh) for hh in heads]

    key_idx = lax.broadcasted_iota(jnp.int32, (T, T), 0)
    qry_idx = lax.broadcasted_iota(jnp.int32, (T, T), 1)
    key_pos = (key_idx + i * T).astype(F32)
    mx0 = []
    for hh in heads:
        s_own = _dot(k_ref[own, head_cols(hh)], qt_ref[head_cols(hh), :])
        s_own = (s_own + sfeat_ref[hh, 6:7, :] * key_pos) * log2e_scale
        s_own = jnp.where(key_idx <= qry_idx, s_own, MOBA_MASK_VALUE)
        s_ref[hh, own_slot, :] = s_own
        mx0.append(_fold_max(s_own))

    def scores(g, mx):
        rows = pl.ds(pl.multiple_of(g * GT, GT), GT)
        kfeat = kfeat_ref[rows, :]
        new = []
        for hh in heads:
            k_aug = jnp.concatenate([k_ref[rows, head_cols(hh)], kfeat], axis=1)
            s = _dot(k_aug, q_aug[hh]) * log2e_scale
            s_ref[hh, rows, :] = s
            new.append(jnp.maximum(mx[hh], _fold_max(s)))
        return tuple(new)

    mx = lax.fori_loop(0, n_groups, scores, tuple(mx0))
    m_scaled = [jnp.max(mx[hh], axis=0, keepdims=True) for hh in heads]

    def weights_t(hh, rows):
        return jnp.exp2(s_ref[hh, rows, :] - m_scaled[hh]).astype(BF16)

    def values_t(hh, cols, width):
        ones = jnp.ones((MOBA_ONES, width), BF16)
        return jnp.concatenate([vt_ref[head_cols(hh), cols], ones], axis=0)

    for hh in heads:
        acc_ref[hh] = _dot(values_t(hh, own, T), weights_t(hh, own_slot))

    def weighted(g, carry):
        rows = pl.ds(pl.multiple_of(g * GT, GT), GT)
        for hh in heads:
            acc_ref[hh] += _dot(values_t(hh, rows, GT), weights_t(hh, rows))
        return carry

    lax.fori_loop(0, n_groups, weighted, 0)
    for hh in heads:
        out_t = acc_ref[hh, :DH, :] / acc_ref[hh, DH:DH + 1, :]
        o_ref[:, head_cols(hh)] = out_t.T.astype(o_ref.dtype)


def _moba(qt, k, vt, kmean, batch, seq, *, name):
    D = k.shape[1]
    n_heads = D // HEAD_DIM
    nb = seq // MOBA_BLOCK
    assert nb % MOBA_GROUP == 0 and nb + 8 <= MOBA_FEATURES and nb % V7X_SUBLANES == 0
    assert n_heads % MOBA_HEADS == 0
    T = MOBA_BLOCK
    W = MOBA_HEADS * HEAD_DIM
    return pl.pallas_call(
        functools.partial(_moba_kernel, n_blocks=nb),
        out_shape=jax.ShapeDtypeStruct((batch * seq, D), BF16),
        grid=(batch, n_heads // MOBA_HEADS, nb),
        in_specs=[pl.BlockSpec((W, T), lambda b, h, i: (h, b * nb + i)),
                  pl.BlockSpec((seq, W), lambda b, h, i: (b, h)),
                  pl.BlockSpec((W, seq), lambda b, h, i: (h, b)),
                  pl.BlockSpec((nb, W), lambda b, h, i: (b, h)),
                  pl.BlockSpec((seq, MOBA_FEATURES), lambda b, h, i: (0, 0)),
                  pl.BlockSpec((MOBA_HEADS, 8, T), lambda b, h, i: (h, 0, 0))],
        out_specs=pl.BlockSpec((T, W), lambda b, h, i: (b * nb + i, h)),
        scratch_shapes=[pltpu.VMEM((MOBA_HEADS, (nb + 1) * T, T), F32),
                        pltpu.VMEM((MOBA_HEADS, HEAD_DIM + MOBA_ONES, T), F32)],
        compiler_params=_params(("parallel", "parallel", "arbitrary")),
        name=name,
    )(qt, k, vt, kmean, _moba_key_features(seq), _moba_slope_features(n_heads))


def _conv_ffn_block(h_f32, h_bf, w_up, layer, conv_w, conv_b, w_down, g, b, alpha, seq):
    act = _ffn_up(h_bf, w_up, layer, conv_w, conv_b, seq, name=f"ffn_up_{layer}")
    return _matmul_res_ln(act, w_down, layer, h_f32, g, b, alpha, tm=256,
                          name=f"ffn_down_ln_{layer}")


def kernel(x, mamba_w_in, mamba_conv_w, mamba_conv_b, mamba_dt_bias, mamba_a_log, mamba_d,
           mamba_norm_w, mamba_w_out, ffn_w_up, ffn_conv_w, ffn_conv_b, ffn_w_down,
           ln_mix_g, ln_mix_b, ln_ffn_g, ln_ffn_b, w_kv, attn_w_q, attn_w_o):
    batch, seq, d_model = x.shape
    depth = ffn_w_up.shape[0]
    n_a = mamba_w_in.shape[0]
    alpha = (2.0 * depth) ** 0.25
    M = batch * seq

    h = x.reshape(M, d_model)
    h_bf = h.astype(BF16)
    w_down_bf = ffn_w_down.astype(BF16)
    k = vt = kmean = None
    for layer in range(depth):
        if layer < n_a:
            d_inner = mamba_norm_w.shape[1]
            n_zx = mamba_w_in.shape[2] - mamba_dt_bias.shape[1]
            z_act = _proj(h_bf, mamba_w_in, layer, 0, d_inner, BF16, silu=True,
                          name=f"in_proj_z_{layer}")
            xbc_act = _matmul_conv_silu(h_bf, mamba_w_in, layer, d_inner,
                                        mamba_conv_w[layer], mamba_conv_b[layer], seq,
                                        name=f"in_proj_xbc_{layer}")
            w_dt = mamba_w_in[layer:layer + 1, :, n_zx:]
            dt_raw = _proj(h_bf, w_dt, 0, 0, w_dt.shape[2], F32, name=f"dt_proj_{layer}")
            y = _ssd(z_act, xbc_act, dt_raw, mamba_dt_bias[layer], mamba_a_log[layer],
                     mamba_d[layer], mamba_norm_w[layer], batch, seq, name=f"ssd_{layer}")
            mix_in, w_mix, mix_layer = y, mamba_w_out, layer
        else:
            j = layer - n_a
            qt = _proj(h_bf, attn_w_q, j, 0, d_model, BF16, transpose_out=True,
                       name=f"q_proj_{j}")
            o = _moba(qt, k, vt, kmean, batch, seq, name=f"moba_{j}")
            mix_in, w_mix, mix_layer = o, attn_w_o, j
        h, h_bf = _matmul_res_ln(mix_in, w_mix.astype(BF16), mix_layer, h, ln_mix_g[layer],
                                 ln_mix_b[layer], alpha, tm=512, name=f"mix_out_ln_{layer}")
        h, h_bf = _conv_ffn_block(h, h_bf, ffn_w_up, layer, ffn_conv_w[layer], ffn_conv_b[layer],
                                  w_down_bf, ln_ffn_g[layer], ln_ffn_b[layer], alpha, seq)
        if layer == n_a - 1:
            w_kv3 = w_kv.reshape(1, d_model, 2 * d_model)
            k, kmean = _proj(h_bf, w_kv3, 0, 0, d_model, BF16, block_means=True, name="k_proj")
            vt = _proj(h_bf, w_kv3, 0, d_model, d_model, BF16, transpose_out=True, name="v_proj")
    return h.reshape(batch, seq, d_model)
```

```python
import functools
import math

import jax
import jax.numpy as jnp
from jax import lax
from jax.experimental import pallas as pl
from jax.experimental.pallas import tpu as pltpu

F32 = jnp.float32
BF16 = jnp.bfloat16

SSM_HEAD_DIM = 64
D_STATE = 128
N_SSM_GROUPS = 8
SSD_CHUNK = 128
HEAD_DIM = 128
MOBA_BLOCK = 256
MOBA_TOPK = 3
LN_EPS = 1e-5
RMS_EPS = 1e-5

V7X_LANES = 128
V7X_SUBLANES = 8
V7X_VMEM_LIMIT_BYTES = 56 * 1024 * 1024


def _tile(dim, target, unit=V7X_LANES):
    if dim <= target:
        return dim
    best = None
    t = unit
    while t <= target:
        if dim % t == 0:
            best = t
        t += unit
    assert best is not None, (dim, target, unit)
    return best


def _params(semantics):
    return pltpu.CompilerParams(dimension_semantics=semantics,
                                vmem_limit_bytes=V7X_VMEM_LIMIT_BYTES)


def _dot(a, b):
    return jnp.dot(a, b, preferred_element_type=F32)


def _dot_nt(a, b):
    return lax.dot_general(a, b, (((1,), (1,)), ((), ())), preferred_element_type=F32)


def _split3(v):
    hi = v.astype(BF16)
    r1 = v - hi.astype(F32)
    mid = r1.astype(BF16)
    lo = (r1 - mid.astype(F32)).astype(BF16)
    return hi, mid, lo


def _silu(v):
    return v * jax.nn.sigmoid(v)


def _cast_weight_once(w_ref, wbf_ref, w_transposed=False):
    @pl.when(pl.program_id(1) == 0)
    def _():
        w = w_ref[...]
        wbf_ref[...] = (w.T if w_transposed else w).astype(BF16)


def _weight_spec(layer, K, tn, off, w_transposed):
    if w_transposed:
        return pl.BlockSpec((None, tn, K), lambda n, m: (layer, n + off, 0))
    return pl.BlockSpec((None, K, tn), lambda n, m: (layer, 0, n + off))


def _proj_kernel(x_ref, w_ref, *refs, silu, transpose_out, block_means, w_transposed):
    if block_means:
        o_ref, km_ref, wbf_ref = refs
    else:
        o_ref, wbf_ref = refs
    _cast_weight_once(w_ref, wbf_ref, w_transposed)
    acc = _dot(x_ref[...], wbf_ref[...])
    if block_means:
        nb, tn = km_ref.shape[1], km_ref.shape[2]
        km_ref[0] = jnp.mean(acc.reshape(nb, MOBA_BLOCK, tn), axis=1)
    if silu:
        acc = _silu(acc)
    out = acc.astype(o_ref.dtype)
    o_ref[...] = out.T if transpose_out else out


def _proj(x, w, layer, col0, n_cols, out_dtype, *, silu=False, transpose_out=False,
          block_means=False, w_transposed=False, tm=1024, tn=1024, name):
    M, K = x.shape
    tm = _tile(M, tm, MOBA_BLOCK if block_means else V7X_LANES)
    tn = _tile(n_cols, tn)
    assert col0 % tn == 0
    off = col0 // tn
    if transpose_out:
        out_shape = [jax.ShapeDtypeStruct((n_cols, M), out_dtype)]
        out_specs = [pl.BlockSpec((tn, tm), lambda n, m: (n, m))]
    else:
        out_shape = [jax.ShapeDtypeStruct((M, n_cols), out_dtype)]
        out_specs = [pl.BlockSpec((tm, tn), lambda n, m: (m, n))]
    if block_means:
        nb = tm // MOBA_BLOCK
        out_shape.append(jax.ShapeDtypeStruct((M // tm, nb, n_cols), F32))
        out_specs.append(pl.BlockSpec((1, nb, tn), lambda n, m: (m, 0, n)))
    outs = pl.pallas_call(
        functools.partial(_proj_kernel, silu=silu, transpose_out=transpose_out,
                          block_means=block_means, w_transposed=w_transposed),
        out_shape=tuple(out_shape),
        grid=(n_cols // tn, M // tm),
        in_specs=[pl.BlockSpec((tm, K), lambda n, m: (m, 0)),
                  _weight_spec(layer, K, tn, off, w_transposed)],
        out_specs=tuple(out_specs),
        scratch_shapes=[pltpu.VMEM((K, tn), BF16)],
        compiler_params=_params(("parallel", "arbitrary")),
        name=name,
    )(x, w)
    if block_means:
        return outs[0], outs[1].reshape(M // MOBA_BLOCK, n_cols)
    return outs[0]


def _mm_ln_kernel(x_ref, w_ref, h_ref, g_ref, b_ref, of_ref, ob_ref, *, alpha):
    y = alpha * h_ref[...] + _dot(x_ref[...], w_ref[...])
    mu = jnp.mean(y, axis=-1, keepdims=True)
    d = y - mu
    var = jnp.mean(d * d, axis=-1, keepdims=True)
    out = d * lax.rsqrt(var + LN_EPS) * g_ref[...] + b_ref[...]
    of_ref[...] = out
    ob_ref[...] = out.astype(BF16)


def _matmul_res_ln(x, w, layer, h, g, b, alpha, *, tm, name):
    M, K = x.shape
    N = w.shape[2]
    tm = _tile(M, tm)
    return pl.pallas_call(
        functools.partial(_mm_ln_kernel, alpha=alpha),
        out_shape=(jax.ShapeDtypeStruct((M, N), F32), jax.ShapeDtypeStruct((M, N), BF16)),
        grid=(M // tm,),
        in_specs=[pl.BlockSpec((tm, K), lambda m: (m, 0)),
                  pl.BlockSpec((None, K, N), lambda m: (layer, 0, 0),
                               pipeline_mode=pl.Buffered(1)),
                  pl.BlockSpec((tm, N), lambda m: (m, 0)),
                  pl.BlockSpec((1, N), lambda m: (0, 0)),
                  pl.BlockSpec((1, N), lambda m: (0, 0))],
        out_specs=(pl.BlockSpec((tm, N), lambda m: (m, 0)),
                   pl.BlockSpec((tm, N), lambda m: (m, 0))),
        compiler_params=_params(("parallel",)),
        name=name,
    )(x, w, h, g.reshape(1, N), b.reshape(1, N))


CONV_COLS = 256
CONV_ROWS = 512


def _runtime_zero_rows(step):
    start = jnp.minimum(step, 0) * V7X_SUBLANES
    return pl.ds(pl.multiple_of(start, V7X_SUBLANES), CONV_ROWS)


def _pipelined_pieces(tile_shape, produce, consume):
    tm, tn = tile_shape
    pieces = [(slice(r, r + CONV_ROWS), slice(c, c + CONV_COLS))
              for c in range(0, tn, CONV_COLS) for r in range(0, tm, CONV_ROWS)]
    produce(0, *pieces[0])
    for j in range(1, len(pieces)):
        produce(j, *pieces[j])
        consume(j - 1, *pieces[j - 1])
    consume(len(pieces) - 1, *pieces[-1])


def _shift_rows(u, tail, k):
    tm, cw = u.shape
    groups = jnp.concatenate([tail, u], axis=0).reshape(tm // V7X_SUBLANES + 1, V7X_SUBLANES, cw)
    rolled = pltpu.roll(groups, k, axis=1)
    sub = lax.broadcasted_iota(jnp.int32, (1, V7X_SUBLANES, cw), 1)
    out = jnp.where(sub < k, rolled[:-1], rolled[1:])
    return out.reshape(tm, cw)


def _conv_rows(u, tail, w_ref, b_ref, cols):
    K = w_ref.shape[0]
    out = b_ref[:, cols] + w_ref[K - 1:K, cols] * u
    for k in range(1, K):
        out = out + w_ref[K - 1 - k:K - k, cols] * _shift_rows(u, tail, k)
    return out


def _mm_conv_silu_kernel(x_ref, w_ref, cw_ref, cb_ref, o_ref, tail_ref, ua_ref, ub_ref, wbf_ref,
                         *, tiles_per_seq, w_transposed):
    m = pl.program_id(1)
    tn = o_ref.shape[1]

    @pl.when(m % tiles_per_seq == 0)
    def _():
        tail_ref[...] = jnp.zeros_like(tail_ref)

    _cast_weight_once(w_ref, wbf_ref, w_transposed)
    bufs = (ua_ref, ub_ref)
    at = _runtime_zero_rows(m)

    def produce(j, rows, cols):
        bufs[j % 2][at, :] = _dot(x_ref[rows, :], wbf_ref[:, cols])

    def consume(j, rows, cols):
        u = bufs[j % 2][at, :]
        out = _silu(_conv_rows(u, tail_ref[:, cols], cw_ref, cb_ref, cols))
        o_ref[rows, cols] = out.astype(o_ref.dtype)
        tail_ref[:, cols] = u[-V7X_SUBLANES:]

    _pipelined_pieces(o_ref.shape, produce, consume)


def _matmul_conv_silu(x, w, layer, col0, conv_w, conv_b, seq, *, w_transposed=False, tm=1024,
                      tn=1024, name):
    M, K = x.shape
    N = conv_w.shape[1]
    tm = _tile(seq, tm, V7X_SUBLANES)
    tn = _tile(N, tn)
    assert col0 % tn == 0
    off = col0 // tn
    kc = conv_w.shape[0]
    return pl.pallas_call(
        functools.partial(_mm_conv_silu_kernel, tiles_per_seq=seq // tm,
                          w_transposed=w_transposed),
        out_shape=jax.ShapeDtypeStruct((M, N), BF16),
        grid=(N // tn, M // tm),
        in_specs=[pl.BlockSpec((tm, K), lambda n, m: (m, 0)),
                  _weight_spec(layer, K, tn, off, w_transposed),
                  pl.BlockSpec((kc, tn), lambda n, m: (0, n)),
                  pl.BlockSpec((1, tn), lambda n, m: (0, n))],
        out_specs=pl.BlockSpec((tm, tn), lambda n, m: (m, n)),
        scratch_shapes=[pltpu.VMEM((V7X_SUBLANES, tn), F32),
                        pltpu.VMEM((CONV_ROWS, CONV_COLS), F32),
                        pltpu.VMEM((CONV_ROWS, CONV_COLS), F32),
                        pltpu.VMEM((K, tn), BF16)],
        compiler_params=_params(("parallel", "arbitrary")),
        name=name,
    )(x, w, conv_w, conv_b.reshape(1, N))


def _ffn_up_kernel(x_ref, wg_ref, wv_ref, cwg_ref, cwv_ref, cbg_ref, cbv_ref, o_ref,
                   tailg_ref, tailv_ref, wgbf_ref, wvbf_ref, *, tiles_per_seq):
    m = pl.program_id(1)

    @pl.when(m % tiles_per_seq == 0)
    def _():
        tailg_ref[...] = jnp.zeros_like(tailg_ref)
        tailv_ref[...] = jnp.zeros_like(tailv_ref)

    _cast_weight_once(wg_ref, wgbf_ref)
    _cast_weight_once(wv_ref, wvbf_ref)
    x = x_ref[...]
    ug = _dot(x, wgbf_ref[...])
    uv = _dot(x, wvbf_ref[...])
    cols = slice(None)
    gate = _conv_rows(ug, tailg_ref[...], cwg_ref, cbg_ref, cols)
    val = _conv_rows(uv, tailv_ref[...], cwv_ref, cbv_ref, cols)
    o_ref[...] = (_silu(gate) * val).astype(o_ref.dtype)
    tailg_ref[...] = ug[-V7X_SUBLANES:]
    tailv_ref[...] = uv[-V7X_SUBLANES:]


def _ffn_up(x, w_up, layer, conv_w, conv_b, seq, *, tm=1024, tn=512, name):
    M, K = x.shape
    d_ff = w_up.shape[2] // 2
    tm = _tile(seq, tm, V7X_SUBLANES)
    tn = _tile(d_ff, tn)
    nn = d_ff // tn
    kc = conv_w.shape[0]
    conv_b = conv_b.reshape(1, 2 * d_ff)
    return pl.pallas_call(
        functools.partial(_ffn_up_kernel, tiles_per_seq=seq // tm),
        out_shape=jax.ShapeDtypeStruct((M, d_ff), BF16),
        grid=(nn, M // tm),
        in_specs=[pl.BlockSpec((tm, K), lambda n, m: (m, 0)),
                  pl.BlockSpec((None, K, tn), lambda n, m: (layer, 0, n)),
                  pl.BlockSpec((None, K, tn), lambda n, m: (layer, 0, n + nn)),
                  pl.BlockSpec((kc, tn), lambda n, m: (0, n)),
                  pl.BlockSpec((kc, tn), lambda n, m: (0, n + nn)),
                  pl.BlockSpec((1, tn), lambda n, m: (0, n)),
                  pl.BlockSpec((1, tn), lambda n, m: (0, n + nn))],
        out_specs=pl.BlockSpec((tm, tn), lambda n, m: (m, n)),
        scratch_shapes=[pltpu.VMEM((V7X_SUBLANES, tn), F32),
                        pltpu.VMEM((V7X_SUBLANES, tn), F32),
                        pltpu.VMEM((K, tn), BF16),
                        pltpu.VMEM((K, tn), BF16)],
        compiler_params=_params(("parallel", "arbitrary")),
        name=name,
    )(x, w_up, w_up, conv_w, conv_w, conv_b, conv_b)


def _ssd_kernel(z_ref, xs_ref, bc_ref, dt_ref, dtb_ref, alog_ref, dskip_ref, nw_ref, expand_ref,
                o_ref, state_ref, acum_t_ref, y_ref, *, heads_per_group):
    c = pl.program_id(1)
    L = SSD_CHUNK
    P = SSM_HEAD_DIM
    N = D_STATE
    G = N_SSM_GROUPS
    R = heads_per_group
    H = G * R
    GP = R * P

    @pl.when(c == 0)
    def _():
        state_ref[...] = jnp.zeros_like(state_ref)

    dt = jax.nn.softplus(dt_ref[...] + dtb_ref[...])
    adt = dt * (-jnp.exp(alog_ref[...]))

    row = lax.broadcasted_iota(jnp.int32, (L, L), 0)
    col = lax.broadcasted_iota(jnp.int32, (L, L), 1)
    causal = row >= col
    tril = jnp.where(causal, 1.0, 0.0).astype(BF16)
    acum = _dot(jnp.concatenate([tril, tril, tril], axis=1),
                jnp.concatenate(_split3(adt), axis=0))

    eye = jnp.where(lax.broadcasted_iota(jnp.int32, (H, H), 0)
                    == lax.broadcasted_iota(jnp.int32, (H, H), 1), 1.0, 0.0).astype(BF16)
    acum_t_ref[...] = _dot_nt(jnp.concatenate([eye, eye, eye], axis=1),
                              jnp.concatenate(_split3(acum), axis=1))

    acum_last = acum[L - 1:L, :]
    decay_to_end = jnp.exp(acum_last - acum)
    decay_from_start = jnp.exp(acum)
    chunk_decay = jnp.broadcast_to(jnp.exp(acum_last), (V7X_SUBLANES, H))

    stacked = jnp.concatenate([dt, decay_to_end, decay_from_start, chunk_decay], axis=0)
    wide = _dot(jnp.concatenate(_split3(stacked), axis=1), expand_ref[...])
    dt_w = wide[0:L]
    dte_w = wide[L:2 * L]
    dfs_w = wide[2 * L:3 * L]
    cdecay_w = wide[3 * L:3 * L + 1]

    xs = xs_ref[...].astype(F32)
    xdt = xs * dt_w
    xdt_bf = xdt.astype(BF16)
    xend_bf = (xdt * dte_w).astype(BF16)
    low_half = lax.broadcasted_iota(jnp.int32, (L, 2 * P), 1) < P

    for g in range(G):
        gs = slice(g * GP, (g + 1) * GP)
        c_g = bc_ref[:, G * N + g * N:G * N + (g + 1) * N]
        b_gt = bc_ref[:, g * N:(g + 1) * N].astype(F32).T.astype(BF16)
        cb = _dot(c_g, b_gt)
        state_g = state_ref[:, gs]
        y_off = _dot(c_g, state_g.astype(BF16)) * dfs_w[:, gs]
        state_ref[:, gs] = state_g * cdecay_w[:, gs] + _dot(b_gt, xend_bf[:, gs])
        for pair in range(R // 2):
            h0 = g * R + 2 * pair
            m_pair = []
            for h in (h0, h0 + 1):
                seg = acum[:, h:h + 1] - acum_t_ref[h:h + 1, :]
                m_pair.append((cb * jnp.exp(jnp.where(causal, seg, -jnp.inf))).astype(BF16))
            x_pair = xdt_bf[:, h0 * P:(h0 + 2) * P]
            zero = jnp.zeros_like(x_pair)
            rhs = jnp.concatenate([jnp.where(low_half, x_pair, zero),
                                   jnp.where(low_half, zero, x_pair)], axis=0)
            y_ref[:, h0 * P:(h0 + 2) * P] = _dot(jnp.concatenate(m_pair, axis=1), rhs)
        y = y_ref[:, gs] + y_off + dskip_ref[:, gs] * xs[:, gs]
        y = y * z_ref[:, gs].astype(F32)
        ms = jnp.mean(y * y, axis=-1, keepdims=True)
        o_ref[:, gs] = (y * lax.rsqrt(ms + RMS_EPS) * nw_ref[:, gs]).astype(o_ref.dtype)


def _ssd(z_act, xbc_act, dt_raw, dt_bias, a_log, d_skip, norm_w, batch, seq, *, name):
    M, d_inner = z_act.shape
    H = dt_bias.shape[0]
    R = H // N_SSM_GROUPS
    gn2 = 2 * N_SSM_GROUPS * D_STATE
    assert d_inner == H * SSM_HEAD_DIM and d_inner % gn2 == 0 and seq % SSD_CHUNK == 0
    assert R % 2 == 0
    L = SSD_CHUNK
    nc = seq // L
    expand = jnp.repeat(jnp.eye(H, dtype=BF16), SSM_HEAD_DIM, axis=1)
    expand3 = jnp.concatenate([expand, expand, expand], axis=0)
    d_wide = jnp.repeat(d_skip.astype(F32), SSM_HEAD_DIM).reshape(1, d_inner)
    row = lambda b, c: b * nc + c
    const = lambda b, c: (0, 0)
    return pl.pallas_call(
        functools.partial(_ssd_kernel, heads_per_group=R),
        out_shape=jax.ShapeDtypeStruct((M, d_inner), BF16),
        grid=(batch, nc),
        in_specs=[pl.BlockSpec((L, d_inner), lambda b, c: (row(b, c), 0)),
                  pl.BlockSpec((L, d_inner), lambda b, c: (row(b, c), 0)),
                  pl.BlockSpec((L, gn2), lambda b, c: (row(b, c), d_inner // gn2)),
                  pl.BlockSpec((L, H), lambda b, c: (row(b, c), 0)),
                  pl.BlockSpec((1, H), const),
                  pl.BlockSpec((1, H), const),
                  pl.BlockSpec((1, d_inner), const),
                  pl.BlockSpec((1, d_inner), const),
                  pl.BlockSpec((3 * H, d_inner), const)],
        out_specs=pl.BlockSpec((L, d_inner), lambda b, c: (row(b, c), 0)),
        scratch_shapes=[pltpu.VMEM((D_STATE, d_inner), F32),
                        pltpu.VMEM((H, L), F32),
                        pltpu.VMEM((L, d_inner), F32)],
        compiler_params=_params(("parallel", "arbitrary")),
        name=name,
    )(z_act, xbc_act, xbc_act, dt_raw,
      dt_bias.reshape(1, H).astype(F32), a_log.reshape(1, H).astype(F32), d_wide,
      norm_w.reshape(1, d_inner).astype(F32), expand3)


MOBA_MASK_VALUE = -1e30
MOBA_FEATURES = 128
MOBA_GROUP = 4
MOBA_HEADS = 4
MOBA_ONES = 16


def _moba_key_features(seq):
    nb = seq // MOBA_BLOCK
    pos = jnp.arange(seq)
    blk = pos // MOBA_BLOCK
    onehot = (blk[:, None] == jnp.arange(nb)[None, :]).astype(F32)
    coarse = jnp.broadcast_to((blk * MOBA_BLOCK).astype(F32)[:, None], (seq, 3))
    fine = jnp.broadcast_to((pos % MOBA_BLOCK).astype(F32)[:, None], (seq, 3))
    zeros = jnp.zeros((seq, MOBA_FEATURES - nb - 6), F32)
    return jnp.concatenate([onehot, coarse, fine, zeros], axis=1).astype(BF16)


def _moba_slope_features(n_heads):
    slope = 2.0 ** (-8.0 * jnp.arange(1, n_heads + 1, dtype=F32) / n_heads) / (HEAD_DIM ** -0.5)
    hi = slope.astype(BF16).astype(F32)
    mid = (slope - hi).astype(BF16).astype(F32)
    lo = (slope - hi - mid).astype(BF16).astype(F32)
    rows = jnp.stack([hi, mid, lo, hi, mid, lo, slope, jnp.zeros_like(slope)], axis=1)
    return jnp.broadcast_to(rows[:, :, None], (n_heads, 8, MOBA_BLOCK))


def _fold_max(s):
    return jnp.max(s.reshape(s.shape[0] // V7X_SUBLANES, V7X_SUBLANES, s.shape[1]), axis=0)


def _fold_sum(s):
    return jnp.sum(s.reshape(s.shape[0] // V7X_SUBLANES, V7X_SUBLANES, s.shape[1]), axis=0)


def _moba_kernel(qt_ref, k_ref, vt_ref, km_ref, kfeat_ref, sfeat_ref, o_ref,
                 s_ref, acc_ref, *, n_blocks):
    i = pl.program_id(2)
    T = MOBA_BLOCK
    GT = MOBA_GROUP * T
    DH = HEAD_DIM
    NB = n_blocks
    heads = range(MOBA_HEADS)
    log2e_scale = (HEAD_DIM ** -0.5) * math.log2(math.e)
    own = pl.ds(pl.multiple_of(i * T, T), T)
    own_slot = pl.ds(NB * T, T)
    n_groups = (i + MOBA_GROUP - 1) // MOBA_GROUP

    def head_cols(hh):
        return slice(hh * DH, (hh + 1) * DH)

    def select(hh):
        qt = qt_ref[head_cols(hh), :]
        gate = _dot(km_ref[:, head_cols(hh)].astype(BF16), qt)
        blk = lax.broadcasted_iota(jnp.int32, gate.shape, 0).astype(F32)
        valid = blk < i.astype(F32)
        neg_inf = jnp.float32(-jnp.inf)
        chosen = jnp.zeros(gate.shape, dtype=jnp.bool_)
        for _ in range(min(MOBA_TOPK, NB)):
            cur = jnp.where(jnp.logical_and(valid, jnp.logical_not(chosen)), gate, neg_inf)
            best = jnp.max(cur, axis=0, keepdims=True)
            hit = jnp.logical_and(cur == best, cur > neg_inf)
            first = jnp.min(jnp.where(hit, blk, float(NB)), axis=0, keepdims=True)
            chosen = jnp.logical_or(chosen, blk == first)
        mask_rows = jnp.where(chosen, 0.0, MOBA_MASK_VALUE)
        feat = jnp.concatenate(
            [mask_rows, sfeat_ref[hh], jnp.zeros((MOBA_FEATURES - NB - 8, T), F32)], axis=0)
        return jnp.concatenate([qt, feat.astype(BF16)], axis=0)

    q_aug = [select(hh) for hh in heads]

    key_idx = lax.broadcasted_iota(jnp.int32, (T, T), 0)
    qry_idx = lax.broadcasted_iota(jnp.int32, (T, T), 1)
    key_pos = (key_idx + i * T).astype(F32)
    mx0 = []
    for hh in heads:
        s_own = _dot(k_ref[own, head_cols(hh)], qt_ref[head_cols(hh), :])
        s_own = (s_own + sfeat_ref[hh, 6:7, :] * key_pos) * log2e_scale
        s_own = jnp.where(key_idx <= qry_idx, s_own, MOBA_MASK_VALUE)
        s_ref[hh, own_slot, :] = s_own
        mx0.append(_fold_max(s_own))

    def scores(g, mx):
        rows = pl.ds(pl.multiple_of(g * GT, GT), GT)
        kfeat = kfeat_ref[rows, :]
        new = []
        for hh in heads:
            k_aug = jnp.concatenate([k_ref[rows, head_cols(hh)], kfeat], axis=1)
            s = _dot(k_aug, q_aug[hh]) * log2e_scale
            s_ref[hh, rows, :] = s
            new.append(jnp.maximum(mx[hh], _fold_max(s)))
        return tuple(new)

    mx = lax.fori_loop(0, n_groups, scores, tuple(mx0))
    m_scaled = [jnp.max(mx[hh], axis=0, keepdims=True) for hh in heads]

    def weights_t(hh, rows):
        return jnp.exp2(s_ref[hh, rows, :] - m_scaled[hh]).astype(BF16)

    def values_t(hh, cols, width):
        ones = jnp.ones((MOBA_ONES, width), BF16)
        return jnp.concatenate([vt_ref[head_cols(hh), cols], ones], axis=0)

    for hh in heads:
        acc_ref[hh] = _dot(values_t(hh, own, T), weights_t(hh, own_slot))

    def weighted(g, carry):
        rows = pl.ds(pl.multiple_of(g * GT, GT), GT)
        for hh in heads:
            acc_ref[hh] += _dot(values_t(hh, rows, GT), weights_t(hh, rows))
        return carry

    lax.fori_loop(0, n_groups, weighted, 0)
    for hh in heads:
        out_t = acc_ref[hh, :DH, :] / acc_ref[hh, DH:DH + 1, :]
        o_ref[:, head_cols(hh)] = out_t.T.astype(o_ref.dtype)


def _moba(qt, k, vt, kmean, batch, seq, *, name):
    D = k.shape[1]
    n_heads = D // HEAD_DIM
    nb = seq // MOBA_BLOCK
    assert nb % MOBA_GROUP == 0 and nb + 8 <= MOBA_FEATURES and nb % V7X_SUBLANES == 0
    assert n_heads % MOBA_HEADS == 0
    T = MOBA_BLOCK
    W = MOBA_HEADS * HEAD_DIM
    return pl.pallas_call(
        functools.partial(_moba_kernel, n_blocks=nb),
        out_shape=jax.ShapeDtypeStruct((batch * seq, D), BF16),
        grid=(batch, n_heads // MOBA_HEADS, nb),
        in_specs=[pl.BlockSpec((W, T), lambda b, h, i: (h, b * nb + i)),
                  pl.BlockSpec((seq, W), lambda b, h, i: (b, h)),
                  pl.BlockSpec((W, seq), lambda b, h, i: (h, b)),
                  pl.BlockSpec((nb, W), lambda b, h, i: (b, h)),
                  pl.BlockSpec((seq, MOBA_FEATURES), lambda b, h, i: (0, 0)),
                  pl.BlockSpec((MOBA_HEADS, 8, T), lambda b, h, i: (h, 0, 0))],
        out_specs=pl.BlockSpec((T, W), lambda b, h, i: (b * nb + i, h)),
        scratch_shapes=[pltpu.VMEM((MOBA_HEADS, (nb + 1) * T, T), F32),
                        pltpu.VMEM((MOBA_HEADS, HEAD_DIM + MOBA_ONES, T), F32)],
        compiler_params=_params(("parallel", "parallel", "arbitrary")),
        name=name,
    )(qt, k, vt, kmean, _moba_key_features(seq), _moba_slope_features(n_heads))


def _conv_ffn_block(h_f32, h_bf, w_up, layer, conv_w, conv_b, w_down, g, b, alpha, seq):
    act = _ffn_up(h_bf, w_up, layer, conv_w, conv_b, seq, name=f"ffn_up_{layer}")
    return _matmul_res_ln(act, w_down, layer, h_f32, g, b, alpha, tm=256,
                          name=f"ffn_down_ln_{layer}")


def kernel(x, mamba_w_in, mamba_conv_w, mamba_conv_b, mamba_dt_bias, mamba_a_log, mamba_d,
           mamba_norm_w, mamba_w_out, ffn_w_up, ffn_conv_w, ffn_conv_b, ffn_w_down,
           ln_mix_g, ln_mix_b, ln_ffn_g, ln_ffn_b, w_kv, attn_w_q, attn_w_o):
    batch, seq, d_model = x.shape
    depth = ffn_w_up.shape[0]
    n_a = mamba_w_in.shape[0]
    alpha = (2.0 * depth) ** 0.25
    M = batch * seq

    h = x.reshape(M, d_model)
    h_bf = h.astype(BF16)
    w_down_bf = ffn_w_down.astype(BF16)
    k = vt = kmean = None
    for layer in range(depth):
        if layer < n_a:
            d_inner = mamba_norm_w.shape[1]
            n_zx = mamba_w_in.shape[2] - mamba_dt_bias.shape[1]
            w_in_t = jnp.swapaxes(mamba_w_in, 1, 2)
            z_act = _proj(h_bf, w_in_t, layer, 0, d_inner, BF16, silu=True, w_transposed=True,
                          name=f"in_proj_z_{layer}")
            xbc_act = _matmul_conv_silu(h_bf, w_in_t, layer, d_inner,
                                        mamba_conv_w[layer], mamba_conv_b[layer], seq,
                                        w_transposed=True, name=f"in_proj_xbc_{layer}")
            w_dt = mamba_w_in[layer:layer + 1, :, n_zx:]
            dt_raw = _proj(h_bf, w_dt, 0, 0, w_dt.shape[2], F32, name=f"dt_proj_{layer}")
            y = _ssd(z_act, xbc_act, dt_raw, mamba_dt_bias[layer], mamba_a_log[layer],
                     mamba_d[layer], mamba_norm_w[layer], batch, seq, name=f"ssd_{layer}")
            mix_in, w_mix, mix_layer = y, mamba_w_out, layer
        else:
            j = layer - n_a
            qt = _proj(h_bf, attn_w_q, j, 0, d_model, BF16, transpose_out=True,
                       name=f"q_proj_{j}")
            o = _moba(qt, k, vt, kmean, batch, seq, name=f"moba_{j}")
            mix_in, w_mix, mix_layer = o, attn_w_o, j
        h, h_bf = _matmul_res_ln(mix_in, w_mix.astype(BF16), mix_layer, h, ln_mix_g[layer],
                                 ln_mix_b[layer], alpha, tm=512, name=f"mix_out_ln_{layer}")
        h, h_bf = _conv_ffn_block(h, h_bf, ffn_w_up, layer, ffn_conv_w[layer], ffn_conv_b[layer],
                                  w_down_bf, ln_ffn_g[layer], ln_ffn_b[layer], alpha, seq)
        if layer == n_a - 1:
            w_kv3 = w_kv.reshape(1, d_model, 2 * d_model)
            k, kmean = _proj(h_bf, w_kv3, 0, 0, d_model, BF16, block_means=True, name="k_proj")
            vt = _proj(h_bf, w_kv3, 0, d_model, d_model, BF16, transpose_out=True, name="v_proj")
    return h.reshape(batch, seq, d_model)
```

```python
import functools
import math

import jax
import jax.numpy as jnp
from jax import lax
from jax.experimental import pallas as pl
from jax.experimental.pallas import tpu as pltpu

F32 = jnp.float32
BF16 = jnp.bfloat16

SSM_HEAD_DIM = 64
D_STATE = 128
N_SSM_GROUPS = 8
SSD_CHUNK = 128
HEAD_DIM = 128
MOBA_BLOCK = 256
MOBA_TOPK = 3
LN_EPS = 1e-5
RMS_EPS = 1e-5

V7X_LANES = 128
V7X_SUBLANES = 8
V7X_VMEM_LIMIT_BYTES = 56 * 1024 * 1024


def _tile(dim, target, unit=V7X_LANES):
    if dim <= target:
        return dim
    best = None
    t = unit
    while t <= target:
        if dim % t == 0:
            best = t
        t += unit
    assert best is not None, (dim, target, unit)
    return best


def _params(semantics):
    return pltpu.CompilerParams(dimension_semantics=semantics,
                                vmem_limit_bytes=V7X_VMEM_LIMIT_BYTES)


def _dot(a, b):
    return jnp.dot(a, b, preferred_element_type=F32)


def _dot_nt(a, b):
    return lax.dot_general(a, b, (((1,), (1,)), ((), ())), preferred_element_type=F32)


def _split3(v):
    hi = v.astype(BF16)
    r1 = v - hi.astype(F32)
    mid = r1.astype(BF16)
    lo = (r1 - mid.astype(F32)).astype(BF16)
    return hi, mid, lo


def _silu(v):
    return v * jax.nn.sigmoid(v)


def _cast_weight_once(w_ref, wbf_ref, w_transposed=False):
    @pl.when(pl.program_id(1) == 0)
    def _():
        w = w_ref[...]
        wbf_ref[...] = (w.T if w_transposed else w).astype(BF16)


def _weight_spec(layer, K, tn, off, w_transposed):
    if w_transposed:
        return pl.BlockSpec((None, tn, K), lambda n, m: (layer, n + off, 0))
    return pl.BlockSpec((None, K, tn), lambda n, m: (layer, 0, n + off))


def _proj_kernel(x_ref, w_ref, *refs, silu, transpose_out, block_means, w_transposed):
    if block_means:
        o_ref, km_ref, wbf_ref = refs
    else:
        o_ref, wbf_ref = refs
    _cast_weight_once(w_ref, wbf_ref, w_transposed)
    acc = _dot(x_ref[...], wbf_ref[...])
    if block_means:
        nb, tn = km_ref.shape[1], km_ref.shape[2]
        km_ref[0] = jnp.mean(acc.reshape(nb, MOBA_BLOCK, tn), axis=1)
    if silu:
        acc = _silu(acc)
    out = acc.astype(o_ref.dtype)
    o_ref[...] = out.T if transpose_out else out


def _proj(x, w, layer, col0, n_cols, out_dtype, *, silu=False, transpose_out=False,
          block_means=False, w_transposed=False, tm=1024, tn=1024, name):
    M, K = x.shape
    tm = _tile(M, tm, MOBA_BLOCK if block_means else V7X_LANES)
    tn = _tile(n_cols, tn)
    assert col0 % tn == 0
    off = col0 // tn
    if transpose_out:
        out_shape = [jax.ShapeDtypeStruct((n_cols, M), out_dtype)]
        out_specs = [pl.BlockSpec((tn, tm), lambda n, m: (n, m))]
    else:
        out_shape = [jax.ShapeDtypeStruct((M, n_cols), out_dtype)]
        out_specs = [pl.BlockSpec((tm, tn), lambda n, m: (m, n))]
    if block_means:
        nb = tm // MOBA_BLOCK
        out_shape.append(jax.ShapeDtypeStruct((M // tm, nb, n_cols), F32))
        out_specs.append(pl.BlockSpec((1, nb, tn), lambda n, m: (m, 0, n)))
    outs = pl.pallas_call(
        functools.partial(_proj_kernel, silu=silu, transpose_out=transpose_out,
                          block_means=block_means, w_transposed=w_transposed),
        out_shape=tuple(out_shape),
        grid=(n_cols // tn, M // tm),
        in_specs=[pl.BlockSpec((tm, K), lambda n, m: (m, 0)),
                  _weight_spec(layer, K, tn, off, w_transposed)],
        out_specs=tuple(out_specs),
        scratch_shapes=[pltpu.VMEM((K, tn), BF16)],
        compiler_params=_params(("parallel", "arbitrary")),
        name=name,
    )(x, w)
    if block_means:
        return outs[0], outs[1].reshape(M // MOBA_BLOCK, n_cols)
    return outs[0]


def _mm_ln_kernel(x_ref, w_ref, h_ref, g_ref, b_ref, of_ref, ob_ref, *, alpha):
    y = alpha * h_ref[...] + _dot(x_ref[...], w_ref[...])
    mu = jnp.mean(y, axis=-1, keepdims=True)
    d = y - mu
    var = jnp.mean(d * d, axis=-1, keepdims=True)
    out = d * lax.rsqrt(var + LN_EPS) * g_ref[...] + b_ref[...]
    of_ref[...] = out
    ob_ref[...] = out.astype(BF16)


def _matmul_res_ln(x, w, layer, h, g, b, alpha, *, tm, name):
    M, K = x.shape
    N = w.shape[2]
    tm = _tile(M, tm)
    return pl.pallas_call(
        functools.partial(_mm_ln_kernel, alpha=alpha),
        out_shape=(jax.ShapeDtypeStruct((M, N), F32), jax.ShapeDtypeStruct((M, N), BF16)),
        grid=(M // tm,),
        in_specs=[pl.BlockSpec((tm, K), lambda m: (m, 0)),
                  pl.BlockSpec((None, K, N), lambda m: (layer, 0, 0),
                               pipeline_mode=pl.Buffered(1)),
                  pl.BlockSpec((tm, N), lambda m: (m, 0)),
                  pl.BlockSpec((1, N), lambda m: (0, 0)),
                  pl.BlockSpec((1, N), lambda m: (0, 0))],
        out_specs=(pl.BlockSpec((tm, N), lambda m: (m, 0)),
                   pl.BlockSpec((tm, N), lambda m: (m, 0))),
        compiler_params=_params(("parallel",)),
        name=name,
    )(x, w, h, g.reshape(1, N), b.reshape(1, N))


CONV_COLS = 256
CONV_ROWS = 512


def _runtime_zero_rows(step):
    start = jnp.minimum(step, 0) * V7X_SUBLANES
    return pl.ds(pl.multiple_of(start, V7X_SUBLANES), CONV_ROWS)


def _pipelined_pieces(tile_shape, produce, consume):
    tm, tn = tile_shape
    pieces = [(slice(r, r + CONV_ROWS), slice(c, c + CONV_COLS))
              for c in range(0, tn, CONV_COLS) for r in range(0, tm, CONV_ROWS)]
    produce(0, *pieces[0])
    for j in range(1, len(pieces)):
        produce(j, *pieces[j])
        consume(j - 1, *pieces[j - 1])
    consume(len(pieces) - 1, *pieces[-1])


def _shift_rows(u, tail, k):
    tm, cw = u.shape
    groups = jnp.concatenate([tail, u], axis=0).reshape(tm // V7X_SUBLANES + 1, V7X_SUBLANES, cw)
    rolled = pltpu.roll(groups, k, axis=1)
    sub = lax.broadcasted_iota(jnp.int32, (1, V7X_SUBLANES, cw), 1)
    out = jnp.where(sub < k, rolled[:-1], rolled[1:])
    return out.reshape(tm, cw)


def _conv_rows(u, tail, w_ref, b_ref, cols):
    K = w_ref.shape[0]
    out = b_ref[:, cols] + w_ref[K - 1:K, cols] * u
    for k in range(1, K):
        out = out + w_ref[K - 1 - k:K - k, cols] * _shift_rows(u, tail, k)
    return out


def _mm_conv_silu_kernel(x_ref, w_ref, cw_ref, cb_ref, o_ref, tail_ref, ua_ref, ub_ref, wbf_ref,
                         *, tiles_per_seq, w_transposed):
    m = pl.program_id(1)
    tn = o_ref.shape[1]

    @pl.when(m % tiles_per_seq == 0)
    def _():
        tail_ref[...] = jnp.zeros_like(tail_ref)

    _cast_weight_once(w_ref, wbf_ref, w_transposed)
    bufs = (ua_ref, ub_ref)
    at = _runtime_zero_rows(m)

    def produce(j, rows, cols):
        bufs[j % 2][at, :] = _dot(x_ref[rows, :], wbf_ref[:, cols])

    def consume(j, rows, cols):
        u = bufs[j % 2][at, :]
        out = _silu(_conv_rows(u, tail_ref[:, cols], cw_ref, cb_ref, cols))
        o_ref[rows, cols] = out.astype(o_ref.dtype)
        tail_ref[:, cols] = u[-V7X_SUBLANES:]

    _pipelined_pieces(o_ref.shape, produce, consume)


def _matmul_conv_silu(x, w, layer, col0, conv_w, conv_b, seq, *, w_transposed=False, tm=1024,
                      tn=1024, name):
    M, K = x.shape
    N = conv_w.shape[1]
    tm = _tile(seq, tm, V7X_SUBLANES)
    tn = _tile(N, tn)
    assert col0 % tn == 0
    off = col0 // tn
    kc = conv_w.shape[0]
    return pl.pallas_call(
        functools.partial(_mm_conv_silu_kernel, tiles_per_seq=seq // tm,
                          w_transposed=w_transposed),
        out_shape=jax.ShapeDtypeStruct((M, N), BF16),
        grid=(N // tn, M // tm),
        in_specs=[pl.BlockSpec((tm, K), lambda n, m: (m, 0)),
                  _weight_spec(layer, K, tn, off, w_transposed),
                  pl.BlockSpec((kc, tn), lambda n, m: (0, n)),
                  pl.BlockSpec((1, tn), lambda n, m: (0, n))],
        out_specs=pl.BlockSpec((tm, tn), lambda n, m: (m, n)),
        scratch_shapes=[pltpu.VMEM((V7X_SUBLANES, tn), F32),
                        pltpu.VMEM((CONV_ROWS, CONV_COLS), F32),
                        pltpu.VMEM((CONV_ROWS, CONV_COLS), F32),
                        pltpu.VMEM((K, tn), BF16)],
        compiler_params=_params(("parallel", "arbitrary")),
        name=name,
    )(x, w, conv_w, conv_b.reshape(1, N))


def _ffn_up_kernel(x_ref, wg_ref, wv_ref, cwg_ref, cwv_ref, cbg_ref, cbv_ref, o_ref,
                   tailg_ref, tailv_ref, wgbf_ref, wvbf_ref, *, tiles_per_seq):
    m = pl.program_id(1)

    @pl.when(m % tiles_per_seq == 0)
    def _():
        tailg_ref[...] = jnp.zeros_like(tailg_ref)
        tailv_ref[...] = jnp.zeros_like(tailv_ref)

    _cast_weight_once(wg_ref, wgbf_ref)
    _cast_weight_once(wv_ref, wvbf_ref)
    x = x_ref[...]
    ug = _dot(x, wgbf_ref[...])
    uv = _dot(x, wvbf_ref[...])
    cols = slice(None)
    gate = _conv_rows(ug, tailg_ref[...], cwg_ref, cbg_ref, cols)
    val = _conv_rows(uv, tailv_ref[...], cwv_ref, cbv_ref, cols)
    o_ref[...] = (_silu(gate) * val).astype(o_ref.dtype)
    tailg_ref[...] = ug[-V7X_SUBLANES:]
    tailv_ref[...] = uv[-V7X_SUBLANES:]


def _ffn_up(x, w_up, layer, conv_w, conv_b, seq, *, tm=1024, tn=512, name):
    M, K = x.shape
    d_ff = w_up.shape[2] // 2
    tm = _tile(seq, tm, V7X_SUBLANES)
    tn = _tile(d_ff, tn)
    nn = d_ff // tn
    kc = conv_w.shape[0]
    conv_b = conv_b.reshape(1, 2 * d_ff)
    return pl.pallas_call(
        functools.partial(_ffn_up_kernel, tiles_per_seq=seq // tm),
        out_shape=jax.ShapeDtypeStruct((M, d_ff), BF16),
        grid=(nn, M // tm),
        in_specs=[pl.BlockSpec((tm, K), lambda n, m: (m, 0)),
                  pl.BlockSpec((None, K, tn), lambda n, m: (layer, 0, n)),
                  pl.BlockSpec((None, K, tn), lambda n, m: (layer, 0, n + nn)),
                  pl.BlockSpec((kc, tn), lambda n, m: (0, n)),
                  pl.BlockSpec((kc, tn), lambda n, m: (0, n + nn)),
                  pl.BlockSpec((1, tn), lambda n, m: (0, n)),
                  pl.BlockSpec((1, tn), lambda n, m: (0, n + nn))],
        out_specs=pl.BlockSpec((tm, tn), lambda n, m: (m, n)),
        scratch_shapes=[pltpu.VMEM((V7X_SUBLANES, tn), F32),
                        pltpu.VMEM((V7X_SUBLANES, tn), F32),
                        pltpu.VMEM((K, tn), BF16),
                        pltpu.VMEM((K, tn), BF16)],
        compiler_params=_params(("parallel", "arbitrary")),
        name=name,
    )(x, w_up, w_up, conv_w, conv_w, conv_b, conv_b)


SSD_CHUNKS_PER_STEP = 2


def _ssd_kernel(z_ref, xs_ref, bc_ref, dt_ref, dtb_ref, alog_ref, dskip_ref, nw_ref, expand_ref,
                o_ref, state_ref, acum_t_ref, y_ref, *, heads_per_group):
    @pl.when(pl.program_id(1) == 0)
    def _():
        state_ref[...] = jnp.zeros_like(state_ref)

    for sub in range(SSD_CHUNKS_PER_STEP):
        rows = slice(sub * SSD_CHUNK, (sub + 1) * SSD_CHUNK)
        _ssd_chunk(z_ref.at[rows], xs_ref.at[rows], bc_ref.at[rows], dt_ref.at[rows], dtb_ref,
                   alog_ref, dskip_ref, nw_ref, expand_ref, o_ref.at[rows], state_ref,
                   acum_t_ref.at[sub], y_ref.at[sub], heads_per_group=heads_per_group)


def _ssd_chunk(z_ref, xs_ref, bc_ref, dt_ref, dtb_ref, alog_ref, dskip_ref, nw_ref, expand_ref,
               o_ref, state_ref, acum_t_ref, y_ref, *, heads_per_group):
    L = SSD_CHUNK
    P = SSM_HEAD_DIM
    N = D_STATE
    G = N_SSM_GROUPS
    R = heads_per_group
    H = G * R
    GP = R * P

    dt = jax.nn.softplus(dt_ref[...] + dtb_ref[...])
    adt = dt * (-jnp.exp(alog_ref[...]))

    row = lax.broadcasted_iota(jnp.int32, (L, L), 0)
    col = lax.broadcasted_iota(jnp.int32, (L, L), 1)
    causal = row >= col
    tril = jnp.where(causal, 1.0, 0.0).astype(BF16)
    acum = _dot(jnp.concatenate([tril, tril, tril], axis=1),
                jnp.concatenate(_split3(adt), axis=0))

    eye = jnp.where(lax.broadcasted_iota(jnp.int32, (H, H), 0)
                    == lax.broadcasted_iota(jnp.int32, (H, H), 1), 1.0, 0.0).astype(BF16)
    acum_t_ref[...] = _dot_nt(jnp.concatenate([eye, eye, eye], axis=1),
                              jnp.concatenate(_split3(acum), axis=1))

    acum_last = acum[L - 1:L, :]
    decay_to_end = jnp.exp(acum_last - acum)
    decay_from_start = jnp.exp(acum)
    chunk_decay = jnp.broadcast_to(jnp.exp(acum_last), (V7X_SUBLANES, H))

    stacked = jnp.concatenate([dt, decay_to_end, decay_from_start, chunk_decay], axis=0)
    wide = _dot(jnp.concatenate(_split3(stacked), axis=1), expand_ref[...])
    dt_w = wide[0:L]
    dte_w = wide[L:2 * L]
    dfs_w = wide[2 * L:3 * L]
    cdecay_w = wide[3 * L:3 * L + 1]

    xs = xs_ref[...].astype(F32)
    xdt = xs * dt_w
    xdt_bf = xdt.astype(BF16)
    xend_bf = (xdt * dte_w).astype(BF16)
    low_half = lax.broadcasted_iota(jnp.int32, (L, 2 * P), 1) < P

    for g in range(G):
        gs = slice(g * GP, (g + 1) * GP)
        c_g = bc_ref[:, G * N + g * N:G * N + (g + 1) * N]
        b_gt = bc_ref[:, g * N:(g + 1) * N].astype(F32).T.astype(BF16)
        cb = _dot(c_g, b_gt)
        state_g = state_ref[:, gs]
        y_off = _dot(c_g, state_g.astype(BF16)) * dfs_w[:, gs]
        state_ref[:, gs] = state_g * cdecay_w[:, gs] + _dot(b_gt, xend_bf[:, gs])
        for pair in range(R // 2):
            h0 = g * R + 2 * pair
            m_pair = []
            for h in (h0, h0 + 1):
                seg = acum[:, h:h + 1] - acum_t_ref[h:h + 1, :]
                m_pair.append((cb * jnp.exp(jnp.where(causal, seg, -jnp.inf))).astype(BF16))
            x_pair = xdt_bf[:, h0 * P:(h0 + 2) * P]
            zero = jnp.zeros_like(x_pair)
            rhs = jnp.concatenate([jnp.where(low_half, x_pair, zero),
                                   jnp.where(low_half, zero, x_pair)], axis=0)
            y_ref[:, h0 * P:(h0 + 2) * P] = _dot(jnp.concatenate(m_pair, axis=1), rhs)
        y = y_ref[:, gs] + y_off + dskip_ref[:, gs] * xs[:, gs]
        y = y * z_ref[:, gs].astype(F32)
        ms = jnp.mean(y * y, axis=-1, keepdims=True)
        o_ref[:, gs] = (y * lax.rsqrt(ms + RMS_EPS) * nw_ref[:, gs]).astype(o_ref.dtype)


def _ssd(z_act, xbc_act, dt_raw, dt_bias, a_log, d_skip, norm_w, batch, seq, *, name):
    M, d_inner = z_act.shape
    H = dt_bias.shape[0]
    R = H // N_SSM_GROUPS
    gn2 = 2 * N_SSM_GROUPS * D_STATE
    L = SSD_CHUNKS_PER_STEP * SSD_CHUNK
    assert d_inner == H * SSM_HEAD_DIM and d_inner % gn2 == 0 and seq % L == 0
    assert R % 2 == 0
    nc = seq // L
    expand = jnp.repeat(jnp.eye(H, dtype=BF16), SSM_HEAD_DIM, axis=1)
    expand3 = jnp.concatenate([expand, expand, expand], axis=0)
    d_wide = jnp.repeat(d_skip.astype(F32), SSM_HEAD_DIM).reshape(1, d_inner)
    row = lambda b, c: b * nc + c
    const = lambda b, c: (0, 0)
    return pl.pallas_call(
        functools.partial(_ssd_kernel, heads_per_group=R),
        out_shape=jax.ShapeDtypeStruct((M, d_inner), BF16),
        grid=(batch, nc),
        in_specs=[pl.BlockSpec((L, d_inner), lambda b, c: (row(b, c), 0)),
                  pl.BlockSpec((L, d_inner), lambda b, c: (row(b, c), 0)),
                  pl.BlockSpec((L, gn2), lambda b, c: (row(b, c), d_inner // gn2)),
                  pl.BlockSpec((L, H), lambda b, c: (row(b, c), 0)),
                  pl.BlockSpec((1, H), const),
                  pl.BlockSpec((1, H), const),
                  pl.BlockSpec((1, d_inner), const),
                  pl.BlockSpec((1, d_inner), const),
                  pl.BlockSpec((3 * H, d_inner), const)],
        out_specs=pl.BlockSpec((L, d_inner), lambda b, c: (row(b, c), 0)),
        scratch_shapes=[pltpu.VMEM((D_STATE, d_inner), F32),
                        pltpu.VMEM((SSD_CHUNKS_PER_STEP, H, SSD_CHUNK), F32),
                        pltpu.VMEM((SSD_CHUNKS_PER_STEP, SSD_CHUNK, d_inner), F32)],
        compiler_params=_params(("parallel", "arbitrary")),
        name=name,
    )(z_act, xbc_act, xbc_act, dt_raw,
      dt_bias.reshape(1, H).astype(F32), a_log.reshape(1, H).astype(F32), d_wide,
      norm_w.reshape(1, d_inner).astype(F32), expand3)


MOBA_MASK_VALUE = -1e30
MOBA_FEATURES = 128
MOBA_GROUP = 4
MOBA_HEADS = 4
MOBA_ONES = 16


def _moba_key_features(seq):
    nb = seq // MOBA_BLOCK
    pos = jnp.arange(seq)
    blk = pos // MOBA_BLOCK
    onehot = (blk[:, None] == jnp.arange(nb)[None, :]).astype(F32)
    coarse = jnp.broadcast_to((blk * MOBA_BLOCK).astype(F32)[:, None], (seq, 3))
    fine = jnp.broadcast_to((pos % MOBA_BLOCK).astype(F32)[:, None], (seq, 3))
    zeros = jnp.zeros((seq, MOBA_FEATURES - nb - 6), F32)
    return jnp.concatenate([onehot, coarse, fine, zeros], axis=1).astype(BF16)


def _moba_slope_features(n_heads):
    slope = 2.0 ** (-8.0 * jnp.arange(1, n_heads + 1, dtype=F32) / n_heads) / (HEAD_DIM ** -0.5)
    hi = slope.astype(BF16).astype(F32)
    mid = (slope - hi).astype(BF16).astype(F32)
    lo = (slope - hi - mid).astype(BF16).astype(F32)
    rows = jnp.stack([hi, mid, lo, hi, mid, lo, slope, jnp.zeros_like(slope)], axis=1)
    return jnp.broadcast_to(rows[:, :, None], (n_heads, 8, MOBA_BLOCK))


def _fold_max(s):
    return jnp.max(s.reshape(s.shape[0] // V7X_SUBLANES, V7X_SUBLANES, s.shape[1]), axis=0)


def _fold_sum(s):
    return jnp.sum(s.reshape(s.shape[0] // V7X_SUBLANES, V7X_SUBLANES, s.shape[1]), axis=0)


def _moba_kernel(qt_ref, k_ref, vt_ref, km_ref, kfeat_ref, sfeat_ref, o_ref,
                 s_ref, acc_ref, *, n_blocks):
    i = pl.program_id(2)
    T = MOBA_BLOCK
    GT = MOBA_GROUP * T
    DH = HEAD_DIM
    NB = n_blocks
    heads = range(MOBA_HEADS)
    log2e_scale = (HEAD_DIM ** -0.5) * math.log2(math.e)
    own = pl.ds(pl.multiple_of(i * T, T), T)
    own_slot = pl.ds(NB * T, T)
    n_groups = (i + MOBA_GROUP - 1) // MOBA_GROUP

    def head_cols(hh):
        return slice(hh * DH, (hh + 1) * DH)

    def select(hh):
        qt = qt_ref[head_cols(hh), :]
        gate = _dot(km_ref[:, head_cols(hh)].astype(BF16), qt)
        blk = lax.broadcasted_iota(jnp.int32, gate.shape, 0).astype(F32)
        valid = blk < i.astype(F32)
        neg_inf = jnp.float32(-jnp.inf)
        chosen = jnp.zeros(gate.shape, dtype=jnp.bool_)
        for _ in range(min(MOBA_TOPK, NB)):
            cur = jnp.where(jnp.logical_and(valid, jnp.logical_not(chosen)), gate, neg_inf)
            best = jnp.max(cur, axis=0, keepdims=True)
            hit = jnp.logical_and(cur == best, cur > neg_inf)
            first = jnp.min(jnp.where(hit, blk, float(NB)), axis=0, keepdims=True)
            chosen = jnp.logical_or(chosen, blk == first)
        mask_rows = jnp.where(chosen, 0.0, MOBA_MASK_VALUE)
        feat = jnp.concatenate(
            [mask_rows, sfeat_ref[hh], jnp.zeros((MOBA_FEATURES - NB - 8, T), F32)], axis=0)
        return jnp.concatenate([qt, feat.astype(BF16)], axis=0)

    q_aug = [select(hh) for hh in heads]

    key_idx = lax.broadcasted_iota(jnp.int32, (T, T), 0)
    qry_idx = lax.broadcasted_iota(jnp.int32, (T, T), 1)
    key_pos = (key_idx + i * T).astype(F32)
    mx0 = []
    for hh in heads:
        s_own = _dot(k_ref[own, head_cols(hh)], qt_ref[head_cols(hh), :])
        s_own = (s_own + sfeat_ref[hh, 6:7, :] * key_pos) * log2e_scale
        s_own = jnp.where(key_idx <= qry_idx, s_own, MOBA_MASK_VALUE)
        s_ref[hh, own_slot, :] = s_own
        mx0.append(_fold_max(s_own))

    def scores(g, mx):
        rows = pl.ds(pl.multiple_of(g * GT, GT), GT)
        kfeat = kfeat_ref[rows, :]
        new = []
        for hh in heads:
            k_aug = jnp.concatenate([k_ref[rows, head_cols(hh)], kfeat], axis=1)
            s = _dot(k_aug, q_aug[hh]) * log2e_scale
            s_ref[hh, rows, :] = s
            new.append(jnp.maximum(mx[hh], _fold_max(s)))
        return tuple(new)

    mx = lax.fori_loop(0, n_groups, scores, tuple(mx0))
    m_scaled = [jnp.max(mx[hh], axis=0, keepdims=True) for hh in heads]

    def weights_t(hh, rows):
        return jnp.exp2(s_ref[hh, rows, :] - m_scaled[hh]).astype(BF16)

    def values_t(hh, cols, width):
        ones = jnp.ones((MOBA_ONES, width), BF16)
        return jnp.concatenate([vt_ref[head_cols(hh), cols], ones], axis=0)

    for hh in heads:
        acc_ref[hh] = _dot(values_t(hh, own, T), weights_t(hh, own_slot))

    def weighted(g, carry):
        rows = pl.ds(pl.multiple_of(g * GT, GT), GT)
        for hh in heads:
            acc_ref[hh] += _dot(values_t(hh, rows, GT), weights_t(hh, rows))
        return carry

    lax.fori_loop(0, n_groups, weighted, 0)
    for hh in heads:
        out_t = acc_ref[hh, :DH, :] / acc_ref[hh, DH:DH + 1, :]
        o_ref[:, head_cols(hh)] = out_t.T.astype(o_ref.dtype)


def _moba(qt, k, vt, kmean, batch, seq, *, name):
    D = k.shape[1]
    n_heads = D // HEAD_DIM
    nb = seq // MOBA_BLOCK
    assert nb % MOBA_GROUP == 0 and nb + 8 <= MOBA_FEATURES and nb % V7X_SUBLANES == 0
    assert n_heads % MOBA_HEADS == 0
    T = MOBA_BLOCK
    W = MOBA_HEADS * HEAD_DIM
    return pl.pallas_call(
        functools.partial(_moba_kernel, n_blocks=nb),
        out_shape=jax.ShapeDtypeStruct((batch * seq, D), BF16),
        grid=(batch, n_heads // MOBA_HEADS, nb),
        in_specs=[pl.BlockSpec((W, T), lambda b, h, i: (h, b * nb + i)),
                  pl.BlockSpec((seq, W), lambda b, h, i: (b, h)),
                  pl.BlockSpec((W, seq), lambda b, h, i: (h, b)),
                  pl.BlockSpec((nb, W), lambda b, h, i: (b, h)),
                  pl.BlockSpec((seq, MOBA_FEATURES), lambda b, h, i: (0, 0)),
                  pl.BlockSpec((MOBA_HEADS, 8, T), lambda b, h, i: (h, 0, 0))],
        out_specs=pl.BlockSpec((T, W), lambda b, h, i: (b * nb + i, h)),
        scratch_shapes=[pltpu.VMEM((MOBA_HEADS, (nb + 1) * T, T), F32),
                        pltpu.VMEM((MOBA_HEADS, HEAD_DIM + MOBA_ONES, T), F32)],
        compiler_params=_params(("parallel", "parallel", "arbitrary")),
        name=name,
    )(qt, k, vt, kmean, _moba_key_features(seq), _moba_slope_features(n_heads))


def _conv_ffn_block(h_f32, h_bf, w_up, layer, conv_w, conv_b, w_down, g, b, alpha, seq):
    act = _ffn_up(h_bf, w_up, layer, conv_w, conv_b, seq, name=f"ffn_up_{layer}")
    return _matmul_res_ln(act, w_down, layer, h_f32, g, b, alpha, tm=256,
                          name=f"ffn_down_ln_{layer}")


def kernel(x, mamba_w_in, mamba_conv_w, mamba_conv_b, mamba_dt_bias, mamba_a_log, mamba_d,
           mamba_norm_w, mamba_w_out, ffn_w_up, ffn_conv_w, ffn_conv_b, ffn_w_down,
           ln_mix_g, ln_mix_b, ln_ffn_g, ln_ffn_b, w_kv, attn_w_q, attn_w_o):
    batch, seq, d_model = x.shape
    depth = ffn_w_up.shape[0]
    n_a = mamba_w_in.shape[0]
    alpha = (2.0 * depth) ** 0.25
    M = batch * seq

    h = x.reshape(M, d_model)
    h_bf = h.astype(BF16)
    w_down_bf = ffn_w_down.astype(BF16)
    k = vt = kmean = None
    for layer in range(depth):
        if layer < n_a:
            d_inner = mamba_norm_w.shape[1]
            n_zx = mamba_w_in.shape[2] - mamba_dt_bias.shape[1]
            w_in_t = jnp.swapaxes(mamba_w_in, 1, 2)
            z_act = _proj(h_bf, w_in_t, layer, 0, d_inner, BF16, silu=True, w_transposed=True,
                          name=f"in_proj_z_{layer}")
            xbc_act = _matmul_conv_silu(h_bf, w_in_t, layer, d_inner,
                                        mamba_conv_w[layer], mamba_conv_b[layer], seq,
                                        w_transposed=True, name=f"in_proj_xbc_{layer}")
            w_dt = mamba_w_in[layer:layer + 1, :, n_zx:]
            dt_raw = _proj(h_bf, w_dt, 0, 0, w_dt.shape[2], F32, name=f"dt_proj_{layer}")
            y = _ssd(z_act, xbc_act, dt_raw, mamba_dt_bias[layer], mamba_a_log[layer],
                     mamba_d[layer], mamba_norm_w[layer], batch, seq, name=f"ssd_{layer}")
            mix_in, w_mix, mix_layer = y, mamba_w_out, layer
        else:
            j = layer - n_a
            qt = _proj(h_bf, attn_w_q, j, 0, d_model, BF16, transpose_out=True,
                       name=f"q_proj_{j}")
            o = _moba(qt, k, vt, kmean, batch, seq, name=f"moba_{j}")
            mix_in, w_mix, mix_layer = o, attn_w_o, j
        h, h_bf = _matmul_res_ln(mix_in, w_mix.astype(BF16), mix_layer, h, ln_mix_g[layer],
                                 ln_mix_b[layer], alpha, tm=512, name=f"mix_out_ln_{layer}")
        h, h_bf = _conv_ffn_block(h, h_bf, ffn_w_up, layer, ffn_conv_w[layer], ffn_conv_b[layer],
                                  w_down_bf, ln_ffn_g[layer], ln_ffn_b[layer], alpha, seq)
        if layer == n_a - 1:
            w_kv3 = w_kv.reshape(1, d_model, 2 * d_model)
            k, kmean = _proj(h_bf, w_kv3, 0, 0, d_model, BF16, block_means=True, name="k_proj")
            vt = _proj(h_bf, w_kv3, 0, d_model, d_model, BF16, transpose_out=True, name="v_proj")
    return h.reshape(batch, seq, d_model)
```

```python
import functools
import math

import jax
import jax.numpy as jnp
from jax import lax
from jax.experimental import pallas as pl
from jax.experimental.pallas import tpu as pltpu

F32 = jnp.float32
BF16 = jnp.bfloat16

SSM_HEAD_DIM = 64
D_STATE = 128
N_SSM_GROUPS = 8
SSD_CHUNK = 128
HEAD_DIM = 128
MOBA_BLOCK = 256
MOBA_TOPK = 3
LN_EPS = 1e-5
RMS_EPS = 1e-5

V7X_LANES = 128
V7X_SUBLANES = 8
V7X_VMEM_LIMIT_BYTES = 56 * 1024 * 1024


def _tile(dim, target, unit=V7X_LANES):
    if dim <= target:
        return dim
    best = None
    t = unit
    while t <= target:
        if dim % t == 0:
            best = t
        t += unit
    assert best is not None, (dim, target, unit)
    return best


def _params(semantics):
    return pltpu.CompilerParams(dimension_semantics=semantics,
                                vmem_limit_bytes=V7X_VMEM_LIMIT_BYTES)


def _dot(a, b):
    return jnp.dot(a, b, preferred_element_type=F32)


def _dot_nt(a, b):
    return lax.dot_general(a, b, (((1,), (1,)), ((), ())), preferred_element_type=F32)


def _split3(v):
    hi = v.astype(BF16)
    r1 = v - hi.astype(F32)
    mid = r1.astype(BF16)
    lo = (r1 - mid.astype(F32)).astype(BF16)
    return hi, mid, lo


def _silu(v):
    return v * jax.nn.sigmoid(v)


def _cast_weight_once(w_ref, wbf_ref, w_transposed=False):
    @pl.when(pl.program_id(1) == 0)
    def _():
        w = w_ref[...]
        wbf_ref[...] = (w.T if w_transposed else w).astype(BF16)


def _weight_spec(layer, K, tn, off, w_transposed):
    if w_transposed:
        return pl.BlockSpec((None, tn, K), lambda n, m: (layer, n + off, 0))
    return pl.BlockSpec((None, K, tn), lambda n, m: (layer, 0, n + off))


def _proj_kernel(x_ref, w_ref, *refs, silu, transpose_out, block_means, w_transposed):
    if block_means:
        o_ref, km_ref, wbf_ref = refs
    else:
        o_ref, wbf_ref = refs
    _cast_weight_once(w_ref, wbf_ref, w_transposed)
    acc = _dot(x_ref[...], wbf_ref[...])
    if block_means:
        nb, tn = km_ref.shape[1], km_ref.shape[2]
        km_ref[0] = jnp.mean(acc.reshape(nb, MOBA_BLOCK, tn), axis=1)
    if silu:
        acc = _silu(acc)
    out = acc.astype(o_ref.dtype)
    o_ref[...] = out.T if transpose_out else out


def _proj(x, w, layer, col0, n_cols, out_dtype, *, silu=False, transpose_out=False,
          block_means=False, w_transposed=False, tm=1024, tn=1024, name):
    M, K = x.shape
    tm = _tile(M, tm, MOBA_BLOCK if block_means else V7X_LANES)
    tn = _tile(n_cols, tn)
    assert col0 % tn == 0
    off = col0 // tn
    if transpose_out:
        out_shape = [jax.ShapeDtypeStruct((n_cols, M), out_dtype)]
        out_specs = [pl.BlockSpec((tn, tm), lambda n, m: (n, m))]
    else:
        out_shape = [jax.ShapeDtypeStruct((M, n_cols), out_dtype)]
        out_specs = [pl.BlockSpec((tm, tn), lambda n, m: (m, n))]
    if block_means:
        nb = tm // MOBA_BLOCK
        out_shape.append(jax.ShapeDtypeStruct((M // tm, nb, n_cols), F32))
        out_specs.append(pl.BlockSpec((1, nb, tn), lambda n, m: (m, 0, n)))
    outs = pl.pallas_call(
        functools.partial(_proj_kernel, silu=silu, transpose_out=transpose_out,
                          block_means=block_means, w_transposed=w_transposed),
        out_shape=tuple(out_shape),
        grid=(n_cols // tn, M // tm),
        in_specs=[pl.BlockSpec((tm, K), lambda n, m: (m, 0)),
                  _weight_spec(layer, K, tn, off, w_transposed)],
        out_specs=tuple(out_specs),
        scratch_shapes=[pltpu.VMEM((K, tn), BF16)],
        compiler_params=_params(("parallel", "arbitrary")),
        name=name,
    )(x, w)
    if block_means:
        return outs[0], outs[1].reshape(M // MOBA_BLOCK, n_cols)
    return outs[0]


def _mm_ln_kernel(x_ref, w_ref, h_ref, g_ref, b_ref, of_ref, ob_ref, *, alpha):
    y = alpha * h_ref[...] + _dot(x_ref[...], w_ref[...])
    mu = jnp.mean(y, axis=-1, keepdims=True)
    d = y - mu
    var = jnp.mean(d * d, axis=-1, keepdims=True)
    out = d * lax.rsqrt(var + LN_EPS) * g_ref[...] + b_ref[...]
    of_ref[...] = out
    ob_ref[...] = out.astype(BF16)


def _matmul_res_ln(x, w, layer, h, g, b, alpha, *, tm, name):
    M, K = x.shape
    N = w.shape[2]
    tm = _tile(M, tm)
    return pl.pallas_call(
        functools.partial(_mm_ln_kernel, alpha=alpha),
        out_shape=(jax.ShapeDtypeStruct((M, N), F32), jax.ShapeDtypeStruct((M, N), BF16)),
        grid=(M // tm,),
        in_specs=[pl.BlockSpec((tm, K), lambda m: (m, 0)),
                  pl.BlockSpec((None, K, N), lambda m: (layer, 0, 0),
                               pipeline_mode=pl.Buffered(1)),
                  pl.BlockSpec((tm, N), lambda m: (m, 0)),
                  pl.BlockSpec((1, N), lambda m: (0, 0)),
                  pl.BlockSpec((1, N), lambda m: (0, 0))],
        out_specs=(pl.BlockSpec((tm, N), lambda m: (m, 0)),
                   pl.BlockSpec((tm, N), lambda m: (m, 0))),
        compiler_params=_params(("parallel",)),
        name=name,
    )(x, w, h, g.reshape(1, N), b.reshape(1, N))


CONV_COLS = 256
CONV_ROWS = 512


def _runtime_zero_rows(step):
    start = jnp.minimum(step, 0) * V7X_SUBLANES
    return pl.ds(pl.multiple_of(start, V7X_SUBLANES), CONV_ROWS)


def _pipelined_pieces(tile_shape, produce, consume):
    tm, tn = tile_shape
    pieces = [(slice(r, r + CONV_ROWS), slice(c, c + CONV_COLS))
              for c in range(0, tn, CONV_COLS) for r in range(0, tm, CONV_ROWS)]
    produce(0, *pieces[0])
    for j in range(1, len(pieces)):
        produce(j, *pieces[j])
        consume(j - 1, *pieces[j - 1])
    consume(len(pieces) - 1, *pieces[-1])


def _shift_rows(u, tail, k):
    tm, cw = u.shape
    groups = jnp.concatenate([tail, u], axis=0).reshape(tm // V7X_SUBLANES + 1, V7X_SUBLANES, cw)
    rolled = pltpu.roll(groups, k, axis=1)
    sub = lax.broadcasted_iota(jnp.int32, (1, V7X_SUBLANES, cw), 1)
    out = jnp.where(sub < k, rolled[:-1], rolled[1:])
    return out.reshape(tm, cw)


def _conv_rows(u, tail, w_ref, b_ref, cols):
    K = w_ref.shape[0]
    out = b_ref[:, cols] + w_ref[K - 1:K, cols] * u
    for k in range(1, K):
        out = out + w_ref[K - 1 - k:K - k, cols] * _shift_rows(u, tail, k)
    return out


def _mm_conv_silu_kernel(x_ref, w_ref, cw_ref, cb_ref, o_ref, tail_ref, ua_ref, ub_ref, wbf_ref,
                         *, tiles_per_seq, w_transposed):
    m = pl.program_id(1)
    tn = o_ref.shape[1]

    @pl.when(m % tiles_per_seq == 0)
    def _():
        tail_ref[...] = jnp.zeros_like(tail_ref)

    _cast_weight_once(w_ref, wbf_ref, w_transposed)
    bufs = (ua_ref, ub_ref)
    at = _runtime_zero_rows(m)

    def produce(j, rows, cols):
        bufs[j % 2][at, :] = _dot(x_ref[rows, :], wbf_ref[:, cols])

    def consume(j, rows, cols):
        u = bufs[j % 2][at, :]
        out = _silu(_conv_rows(u, tail_ref[:, cols], cw_ref, cb_ref, cols))
        o_ref[rows, cols] = out.astype(o_ref.dtype)
        tail_ref[:, cols] = u[-V7X_SUBLANES:]

    _pipelined_pieces(o_ref.shape, produce, consume)


def _matmul_conv_silu(x, w, layer, col0, conv_w, conv_b, seq, *, w_transposed=False, tm=1024,
                      tn=1024, name):
    M, K = x.shape
    N = conv_w.shape[1]
    tm = _tile(seq, tm, V7X_SUBLANES)
    tn = _tile(N, tn)
    assert col0 % tn == 0
    off = col0 // tn
    kc = conv_w.shape[0]
    return pl.pallas_call(
        functools.partial(_mm_conv_silu_kernel, tiles_per_seq=seq // tm,
                          w_transposed=w_transposed),
        out_shape=jax.ShapeDtypeStruct((M, N), BF16),
        grid=(N // tn, M // tm),
        in_specs=[pl.BlockSpec((tm, K), lambda n, m: (m, 0)),
                  _weight_spec(layer, K, tn, off, w_transposed),
                  pl.BlockSpec((kc, tn), lambda n, m: (0, n)),
                  pl.BlockSpec((1, tn), lambda n, m: (0, n))],
        out_specs=pl.BlockSpec((tm, tn), lambda n, m: (m, n)),
        scratch_shapes=[pltpu.VMEM((V7X_SUBLANES, tn), F32),
                        pltpu.VMEM((CONV_ROWS, CONV_COLS), F32),
                        pltpu.VMEM((CONV_ROWS, CONV_COLS), F32),
                        pltpu.VMEM((K, tn), BF16)],
        compiler_params=_params(("parallel", "arbitrary")),
        name=name,
    )(x, w, conv_w, conv_b.reshape(1, N))


def _ffn_up_kernel(x_ref, wg_ref, wv_ref, cwg_ref, cwv_ref, cbg_ref, cbv_ref, o_ref,
                   tailg_ref, tailv_ref, wgbf_ref, wvbf_ref, *, tiles_per_seq):
    m = pl.program_id(1)

    @pl.when(m % tiles_per_seq == 0)
    def _():
        tailg_ref[...] = jnp.zeros_like(tailg_ref)
        tailv_ref[...] = jnp.zeros_like(tailv_ref)

    _cast_weight_once(wg_ref, wgbf_ref)
    _cast_weight_once(wv_ref, wvbf_ref)
    x = x_ref[...]
    ug = _dot(x, wgbf_ref[...])
    uv = _dot(x, wvbf_ref[...])
    cols = slice(None)
    gate = _conv_rows(ug, tailg_ref[...], cwg_ref, cbg_ref, cols)
    val = _conv_rows(uv, tailv_ref[...], cwv_ref, cbv_ref, cols)
    o_ref[...] = (_silu(gate) * val).astype(o_ref.dtype)
    tailg_ref[...] = ug[-V7X_SUBLANES:]
    tailv_ref[...] = uv[-V7X_SUBLANES:]


def _ffn_up(x, w_up, layer, conv_w, conv_b, seq, *, tm=1024, tn=512, name):
    M, K = x.shape
    d_ff = w_up.shape[2] // 2
    tm = _tile(seq, tm, V7X_SUBLANES)
    tn = _tile(d_ff, tn)
    nn = d_ff // tn
    kc = conv_w.shape[0]
    conv_b = conv_b.reshape(1, 2 * d_ff)
    return pl.pallas_call(
        functools.partial(_ffn_up_kernel, tiles_per_seq=seq // tm),
        out_shape=jax.ShapeDtypeStruct((M, d_ff), BF16),
        grid=(nn, M // tm),
        in_specs=[pl.BlockSpec((tm, K), lambda n, m: (m, 0)),
                  pl.BlockSpec((None, K, tn), lambda n, m: (layer, 0, n)),
                  pl.BlockSpec((None, K, tn), lambda n, m: (layer, 0, n + nn)),
                  pl.BlockSpec((kc, tn), lambda n, m: (0, n)),
                  pl.BlockSpec((kc, tn), lambda n, m: (0, n + nn)),
                  pl.BlockSpec((1, tn), lambda n, m: (0, n)),
                  pl.BlockSpec((1, tn), lambda n, m: (0, n + nn))],
        out_specs=pl.BlockSpec((tm, tn), lambda n, m: (m, n)),
        scratch_shapes=[pltpu.VMEM((V7X_SUBLANES, tn), F32),
                        pltpu.VMEM((V7X_SUBLANES, tn), F32),
                        pltpu.VMEM((K, tn), BF16),
                        pltpu.VMEM((K, tn), BF16)],
        compiler_params=_params(("parallel", "arbitrary")),
        name=name,
    )(x, w_up, w_up, conv_w, conv_w, conv_b, conv_b)


def _ssd_kernel(z_ref, xs_ref, bc_ref, dt_ref, dtb_ref, alog_ref, dskip_ref, nw_ref, expand_ref,
                o_ref, state_ref, acum_t_ref, y_ref, *, heads_per_group):
    c = pl.program_id(1)
    L = SSD_CHUNK
    P = SSM_HEAD_DIM
    N = D_STATE
    G = N_SSM_GROUPS
    R = heads_per_group
    H = G * R
    GP = R * P

    @pl.when(c == 0)
    def _():
        state_ref[...] = jnp.zeros_like(state_ref)

    dt = jax.nn.softplus(dt_ref[...] + dtb_ref[...])
    adt = dt * (-jnp.exp(alog_ref[...]))

    row = lax.broadcasted_iota(jnp.int32, (L, L), 0)
    col = lax.broadcasted_iota(jnp.int32, (L, L), 1)
    causal = row >= col
    tril = jnp.where(causal, 1.0, 0.0).astype(BF16)
    acum = _dot(jnp.concatenate([tril, tril, tril], axis=1),
                jnp.concatenate(_split3(adt), axis=0))

    eye = jnp.where(lax.broadcasted_iota(jnp.int32, (H, H), 0)
                    == lax.broadcasted_iota(jnp.int32, (H, H), 1), 1.0, 0.0).astype(BF16)
    acum_t_ref[...] = _dot_nt(jnp.concatenate([eye, eye, eye], axis=1),
                              jnp.concatenate(_split3(acum), axis=1))

    acum_last = acum[L - 1:L, :]
    decay_to_end = jnp.exp(acum_last - acum)
    decay_from_start = jnp.exp(acum)
    chunk_decay = jnp.broadcast_to(jnp.exp(acum_last), (V7X_SUBLANES, H))

    stacked = jnp.concatenate([dt, decay_to_end, decay_from_start, chunk_decay], axis=0)
    wide = _dot(jnp.concatenate(_split3(stacked), axis=1), expand_ref[...])
    dt_w = wide[0:L]
    dte_w = wide[L:2 * L]
    dfs_w = wide[2 * L:3 * L]
    cdecay_w = wide[3 * L:3 * L + 1]

    xs = xs_ref[...].astype(F32)
    xdt = xs * dt_w
    xdt_bf = xdt.astype(BF16)
    xend_bf = (xdt * dte_w).astype(BF16)
    low_half = lax.broadcasted_iota(jnp.int32, (L, 2 * P), 1) < P

    for g in range(G):
        gs = slice(g * GP, (g + 1) * GP)
        c_g = bc_ref[:, G * N + g * N:G * N + (g + 1) * N]
        b_gt = bc_ref[:, g * N:(g + 1) * N].astype(F32).T.astype(BF16)
        cb = _dot(c_g, b_gt)
        state_g = state_ref[:, gs]
        y_off = _dot(c_g, state_g.astype(BF16)) * dfs_w[:, gs]
        state_ref[:, gs] = state_g * cdecay_w[:, gs] + _dot(b_gt, xend_bf[:, gs])
        for pair in range(R // 2):
            h0 = g * R + 2 * pair
            m_pair = []
            for h in (h0, h0 + 1):
                seg = acum[:, h:h + 1] - acum_t_ref[h:h + 1, :]
                m_pair.append((cb * jnp.exp(jnp.where(causal, seg, -jnp.inf))).astype(BF16))
            x_pair = xdt_bf[:, h0 * P:(h0 + 2) * P]
            zero = jnp.zeros_like(x_pair)
            rhs = jnp.concatenate([jnp.where(low_half, x_pair, zero),
                                   jnp.where(low_half, zero, x_pair)], axis=0)
            y_ref[:, h0 * P:(h0 + 2) * P] = _dot(jnp.concatenate(m_pair, axis=1), rhs)
        y = y_ref[:, gs] + y_off + dskip_ref[:, gs] * xs[:, gs]
        y = y * z_ref[:, gs].astype(F32)
        ms = jnp.mean(y * y, axis=-1, keepdims=True)
        o_ref[:, gs] = (y * lax.rsqrt(ms + RMS_EPS) * nw_ref[:, gs]).astype(o_ref.dtype)


def _ssd(z_act, xbc_act, dt_raw, dt_bias, a_log, d_skip, norm_w, batch, seq, *, name):
    M, d_inner = z_act.shape
    H = dt_bias.shape[0]
    R = H // N_SSM_GROUPS
    gn2 = 2 * N_SSM_GROUPS * D_STATE
    assert d_inner == H * SSM_HEAD_DIM and d_inner % gn2 == 0 and seq % SSD_CHUNK == 0
    assert R % 2 == 0
    L = SSD_CHUNK
    nc = seq // L
    expand = jnp.repeat(jnp.eye(H, dtype=BF16), SSM_HEAD_DIM, axis=1)
    expand3 = jnp.concatenate([expand, expand, expand], axis=0)
    d_wide = jnp.repeat(d_skip.astype(F32), SSM_HEAD_DIM).reshape(1, d_inner)
    row = lambda b, c: b * nc + c
    const = lambda b, c: (0, 0)
    return pl.pallas_call(
        functools.partial(_ssd_kernel, heads_per_group=R),
        out_shape=jax.ShapeDtypeStruct((M, d_inner), BF16),
        grid=(batch, nc),
        in_specs=[pl.BlockSpec((L, d_inner), lambda b, c: (row(b, c), 0)),
                  pl.BlockSpec((L, d_inner), lambda b, c: (row(b, c), 0)),
                  pl.BlockSpec((L, gn2), lambda b, c: (row(b, c), d_inner // gn2)),
                  pl.BlockSpec((L, H), lambda b, c: (row(b, c), 0)),
                  pl.BlockSpec((1, H), const),
                  pl.BlockSpec((1, H), const),
                  pl.BlockSpec((1, d_inner), const),
                  pl.BlockSpec((1, d_inner), const),
                  pl.BlockSpec((3 * H, d_inner), const)],
        out_specs=pl.BlockSpec((L, d_inner), lambda b, c: (row(b, c), 0)),
        scratch_shapes=[pltpu.VMEM((D_STATE, d_inner), F32),
                        pltpu.VMEM((H, L), F32),
                        pltpu.VMEM((L, d_inner), F32)],
        compiler_params=_params(("parallel", "arbitrary")),
        name=name,
    )(z_act, xbc_act, xbc_act, dt_raw,
      dt_bias.reshape(1, H).astype(F32), a_log.reshape(1, H).astype(F32), d_wide,
      norm_w.reshape(1, d_inner).astype(F32), expand3)


MOBA_MASK_VALUE = -1e30
MOBA_FEATURES = 128
MOBA_GROUP = 4
MOBA_HEADS = 4
MOBA_ONES = 16


def _moba_key_features(seq):
    nb = seq // MOBA_BLOCK
    pos = jnp.arange(seq)
    blk = pos // MOBA_BLOCK
    onehot = (blk[:, None] == jnp.arange(nb)[None, :]).astype(F32)
    coarse = jnp.broadcast_to((blk * MOBA_BLOCK).astype(F32)[:, None], (seq, 3))
    fine = jnp.broadcast_to((pos % MOBA_BLOCK).astype(F32)[:, None], (seq, 3))
    zeros = jnp.zeros((seq, MOBA_FEATURES - nb - 6), F32)
    return jnp.concatenate([onehot, coarse, fine, zeros], axis=1).astype(BF16)


def _moba_slope_features(n_heads):
    slope = 2.0 ** (-8.0 * jnp.arange(1, n_heads + 1, dtype=F32) / n_heads) / (HEAD_DIM ** -0.5)
    hi = slope.astype(BF16).astype(F32)
    mid = (slope - hi).astype(BF16).astype(F32)
    lo = (slope - hi - mid).astype(BF16).astype(F32)
    rows = jnp.stack([hi, mid, lo, hi, mid, lo, slope, jnp.zeros_like(slope)], axis=1)
    return jnp.broadcast_to(rows[:, :, None], (n_heads, 8, MOBA_BLOCK))


def _fold_max(s):
    return jnp.max(s.reshape(s.shape[0] // V7X_SUBLANES, V7X_SUBLANES, s.shape[1]), axis=0)


def _moba_kernel(qt_ref, k_ref, vt_ref, km_ref, kfeat_ref, sfeat_ref, o_ref,
                 s_ref, acc_ref, *, n_blocks):
    i = pl.program_id(2)
    T = MOBA_BLOCK
    GT = MOBA_GROUP * T
    DH = HEAD_DIM
    NB = n_blocks
    heads = range(MOBA_HEADS)
    log2e_scale = (HEAD_DIM ** -0.5) * math.log2(math.e)
    own = pl.ds(pl.multiple_of(i * T, T), T)
    own_slot = pl.ds(NB * T, T)
    n_groups = (i + MOBA_GROUP - 1) // MOBA_GROUP

    def head_cols(hh):
        return slice(hh * DH, (hh + 1) * DH)

    def select(hh):
        qt = qt_ref[head_cols(hh), :]
        gate = _dot(km_ref[:, head_cols(hh)].astype(BF16), qt)
        blk = lax.broadcasted_iota(jnp.int32, gate.shape, 0).astype(F32)
        valid = blk < i.astype(F32)
        neg_inf = jnp.float32(-jnp.inf)
        chosen = jnp.zeros(gate.shape, dtype=jnp.bool_)
        for _ in range(min(MOBA_TOPK, NB)):
            cur = jnp.where(jnp.logical_and(valid, jnp.logical_not(chosen)), gate, neg_inf)
            best = jnp.max(cur, axis=0, keepdims=True)
            hit = jnp.logical_and(cur == best, cur > neg_inf)
            first = jnp.min(jnp.where(hit, blk, float(NB)), axis=0, keepdims=True)
            chosen = jnp.logical_or(chosen, blk == first)
        mask_rows = jnp.where(chosen, 0.0, MOBA_MASK_VALUE)
        feat = jnp.concatenate(
            [mask_rows, sfeat_ref[hh], jnp.zeros((MOBA_FEATURES - NB - 8, T), F32)], axis=0)
        return jnp.concatenate([qt, feat.astype(BF16)], axis=0)

    q_aug = [select(hh) for hh in heads]

    key_idx = lax.broadcasted_iota(jnp.int32, (T, T), 0)
    qry_idx = lax.broadcasted_iota(jnp.int32, (T, T), 1)
    key_pos = (key_idx + i * T).astype(F32)
    mx0 = []
    for hh in heads:
        s_own = _dot(k_ref[own, head_cols(hh)], qt_ref[head_cols(hh), :])
        s_own = (s_own + sfeat_ref[hh, 6:7, :] * key_pos) * log2e_scale
        s_own = jnp.where(key_idx <= qry_idx, s_own, MOBA_MASK_VALUE)
        s_ref[hh, own_slot, :] = s_own
        mx0.append(_fold_max(s_own))

    def scores(g, mx):
        rows = pl.ds(pl.multiple_of(g * GT, GT), GT)
        kfeat = kfeat_ref[rows, :]
        new = []
        for hh in heads:
            k_aug = jnp.concatenate([k_ref[rows, head_cols(hh)], kfeat], axis=1)
            s = _dot(k_aug, q_aug[hh]) * log2e_scale
            s_ref[hh, rows, :] = s
            new.append(jnp.maximum(mx[hh], _fold_max(s)))
        return tuple(new)

    mx = lax.fori_loop(0, n_groups, scores, tuple(mx0))
    m_scaled = [jnp.max(mx[hh], axis=0, keepdims=True) for hh in heads]

    def weights_t(hh, rows):
        return jnp.exp2(s_ref[hh, rows, :] - m_scaled[hh]).astype(BF16)

    def values_t(hh, cols, width):
        ones = jnp.ones((MOBA_ONES, width), BF16)
        return jnp.concatenate([vt_ref[head_cols(hh), cols], ones], axis=0)

    for hh in heads:
        acc_ref[hh] = _dot(values_t(hh, own, T), weights_t(hh, own_slot))

    def weighted(g, carry):
        rows = pl.ds(pl.multiple_of(g * GT, GT), GT)
        for hh in heads:
            acc_ref[hh] += _dot(values_t(hh, rows, GT), weights_t(hh, rows))
        return carry

    lax.fori_loop(0, n_groups, weighted, 0)
    for hh in heads:
        out_t = acc_ref[hh, :DH, :] / acc_ref[hh, DH:DH + 1, :]
        o_ref[:, head_cols(hh)] = out_t.T.astype(o_ref.dtype)


def _moba(qt, k, vt, kmean, batch, seq, *, name):
    D = k.shape[1]
    n_heads = D // HEAD_DIM
    nb = seq // MOBA_BLOCK
    assert nb % MOBA_GROUP == 0 and nb + 8 <= MOBA_FEATURES and nb % V7X_SUBLANES == 0
    assert n_heads % MOBA_HEADS == 0
    T = MOBA_BLOCK
    W = MOBA_HEADS * HEAD_DIM
    return pl.pallas_call(
        functools.partial(_moba_kernel, n_blocks=nb),
        out_shape=jax.ShapeDtypeStruct((batch * seq, D), BF16),
        grid=(batch, n_heads // MOBA_HEADS, nb),
        in_specs=[pl.BlockSpec((W, T), lambda b, h, i: (h, b * nb + i)),
                  pl.BlockSpec((seq, W), lambda b, h, i: (b, h)),
                  pl.BlockSpec((W, seq), lambda b, h, i: (h, b)),
                  pl.BlockSpec((nb, W), lambda b, h, i: (b, h)),
                  pl.BlockSpec((seq, MOBA_FEATURES), lambda b, h, i: (0, 0)),
                  pl.BlockSpec((MOBA_HEADS, 8, T), lambda b, h, i: (h, 0, 0))],
        out_specs=pl.BlockSpec((T, W), lambda b, h, i: (b * nb + i, h)),
        scratch_shapes=[pltpu.VMEM((MOBA_HEADS, (nb + 1) * T, T), F32),
                        pltpu.VMEM((MOBA_HEADS, HEAD_DIM + MOBA_ONES, T), F32)],
        compiler_params=_params(("parallel", "parallel", "arbitrary")),
        name=name,
    )(qt, k, vt, kmean, _moba_key_features(seq), _moba_slope_features(n_heads))


def _conv_ffn_block(h_f32, h_bf, w_up, layer, conv_w, conv_b, w_down, g, b, alpha, seq):
    act = _ffn_up(h_bf, w_up, layer, conv_w, conv_b, seq, name=f"ffn_up_{layer}")
    return _matmul_res_ln(act, w_down, layer, h_f32, g, b, alpha, tm=256,
                          name=f"ffn_down_ln_{layer}")


def kernel(x, mamba_w_in, mamba_conv_w, mamba_conv_b, mamba_dt_bias, mamba_a_log, mamba_d,
           mamba_norm_w, mamba_w_out, ffn_w_up, ffn_conv_w, ffn_conv_b, ffn_w_down,
           ln_mix_g, ln_mix_b, ln_ffn_g, ln_ffn_b, w_kv, attn_w_q, attn_w_o):
    batch, seq, d_model = x.shape
    depth = ffn_w_up.shape[0]
    n_a = mamba_w_in.shape[0]
    alpha = (2.0 * depth) ** 0.25
    M = batch * seq

    h = x.reshape(M, d_model)
    h_bf = h.astype(BF16)
    w_down_bf = ffn_w_down.astype(BF16)
    k = vt = kmean = None
    for layer in range(depth):
        if layer < n_a:
            d_inner = mamba_norm_w.shape[1]
            n_zx = mamba_w_in.shape[2] - mamba_dt_bias.shape[1]
            w_in_t = jnp.swapaxes(mamba_w_in, 1, 2)
            z_act = _proj(h_bf, w_in_t, layer, 0, d_inner, BF16, silu=True, w_transposed=True,
                          name=f"in_proj_z_{layer}")
            xbc_act = _matmul_conv_silu(h_bf, w_in_t, layer, d_inner,
                                        mamba_conv_w[layer], mamba_conv_b[layer], seq,
                                        w_transposed=True, name=f"in_proj_xbc_{layer}")
            w_dt = mamba_w_in[layer:layer + 1, :, n_zx:]
            dt_raw = _proj(h_bf, w_dt, 0, 0, w_dt.shape[2], F32, name=f"dt_proj_{layer}")
            y = _ssd(z_act, xbc_act, dt_raw, mamba_dt_bias[layer], mamba_a_log[layer],
                     mamba_d[layer], mamba_norm_w[layer], batch, seq, name=f"ssd_{layer}")
            mix_in, w_mix, mix_layer = y, mamba_w_out, layer
        else:
            j = layer - n_a
            qt = _proj(h_bf, attn_w_q, j, 0, d_model, BF16, transpose_out=True,
                       name=f"q_proj_{j}")
            o = _moba(qt, k, vt, kmean, batch, seq, name=f"moba_{j}")
            mix_in, w_mix, mix_layer = o, attn_w_o, j
        h, h_bf = _matmul_res_ln(mix_in, w_mix.astype(BF16), mix_layer, h, ln_mix_g[layer],
                                 ln_mix_b[layer], alpha, tm=512, name=f"mix_out_ln_{layer}")
        h, h_bf = _conv_ffn_block(h, h_bf, ffn_w_up, layer, ffn_conv_w[layer], ffn_conv_b[layer],
                                  w_down_bf, ln_ffn_g[layer], ln_ffn_b[layer], alpha, seq)
        if layer == n_a - 1:
            w_kv3 = w_kv.reshape(1, d_model, 2 * d_model)
            k, kmean = _proj(h_bf, w_kv3, 0, 0, d_model, BF16, block_means=True, name="k_proj")
            vt = _proj(h_bf, w_kv3, 0, d_model, d_model, BF16, transpose_out=True, name="v_proj")
    return h.reshape(batch, seq, d_model)
```
